```python
import math
import jax, jax.numpy as jnp
from jax import lax
import numpy as np

D_MODEL = 1024
BATCH = 8
SEQ = 2048
DEPTH = 4

GRID_W = 64
CTX_LEN = 256
MIX = 256
N_BRANCH = 4
RET_HEADS = 4
RET_HEAD_DIM = MIX // RET_HEADS
CHUNK = 128
POOL_WINDOWS = (2, 4, 8, 16)
POOL_GROUP = MIX // len(POOL_WINDOWS)
FFT_GROUPS = 4
FFT_GROUP_DIM = MIX // FFT_GROUPS
D_FF = -(-8 * D_MODEL // (3 * 256)) * 256
IN_WIDTH = 9 * MIX + N_BRANCH * D_MODEL
ROPE_BASE = 10000.0
EPS = 1e-6

kernel_name = 'hybrid_retention_conv_fourier_pool_dit_block'


def rmsnorm(x, g):
    xf = x.astype(jnp.float32)
    y = xf * lax.rsqrt(jnp.mean(xf * xf, axis=-1, keepdims=True) + EPS)
    return (y * g.astype(jnp.float32)).astype(x.dtype)


def rope_1d(x, pos):
    half = x.shape[-1] // 2
    freqs = ROPE_BASE ** (-jnp.arange(half, dtype=jnp.float32) / half)
    ang = pos[:, None] * freqs[None, :]
    cos, sin = jnp.cos(ang), jnp.sin(ang)
    x1, x2 = x[..., :half], x[..., half:]
    return jnp.concatenate([x1 * cos - x2 * sin, x1 * sin + x2 * cos], axis=-1)


def rope_2d(x, rows, cols):
    half = x.shape[-1] // 2
    return jnp.concatenate([rope_1d(x[..., :half], rows), rope_1d(x[..., half:], cols)], axis=-1)


def heads(t):
    b, n, _ = t.shape
    return t.reshape(b, n, RET_HEADS, RET_HEAD_DIM).transpose(0, 2, 1, 3).astype(jnp.float32)


def retention_direction(q, k, v, log_g, s0, strict):
    b, h, n, dk = q.shape
    dv = v.shape[-1]
    nc = n // CHUNK
    idx = jnp.arange(CHUNK, dtype=jnp.float32)
    diff = idx[:, None] - idx[None, :]
    mask = (diff > 0) if strict else (diff >= 0)
    intra = jnp.where(mask[None], jnp.exp(log_g[:, None, None] * jnp.maximum(diff, 0.0)[None]), 0.0)
    q_decay = jnp.exp(log_g[:, None] * (idx + 1.0)[None])[None, :, :, None]
    k_decay = jnp.exp(log_g[:, None] * (CHUNK - 1.0 - idx)[None])[None, :, :, None]
    chunk_decay = jnp.exp(log_g * CHUNK)[None, :, None, None]

    def to_chunks(t):
        return jnp.moveaxis(t.reshape(b, h, nc, CHUNK, t.shape[-1]), 2, 0)

    def step(state, inp):
        qc, kc, vc = inp
        scores = jnp.einsum('bhid,bhjd->bhij', qc, kc) * intra[None]
        out = (jnp.einsum('bhij,bhjv->bhiv', scores, vc)
               + jnp.einsum('bhid,bhdv->bhiv', qc, state) * q_decay)
        state = state * chunk_decay + jnp.einsum('bhjd,bhjv->bhdv', kc * k_decay, vc)
        return state, out

    state, ys = lax.scan(step, s0, (to_chunks(q), to_chunks(k), to_chunks(v)))
    return jnp.moveaxis(ys, 0, 2).reshape(b, h, n, dv), state


def bidirectional_retention(q, k, v, log_g, s_fwd0, s_bwd0):
    y_f, s_f = retention_direction(q, k, v, log_g[0], s_fwd0, False)
    flip = lambda t: t[:, :, ::-1]
    y_b, s_b = retention_direction(flip(q), flip(k), flip(v), log_g[1], s_bwd0, True)
    return y_f + flip(y_b), s_f, s_b


def retention_output(y, g):
    mu = jnp.mean(y, axis=-1, keepdims=True)
    var = jnp.mean(jnp.square(y - mu), axis=-1, keepdims=True)
    y = (y - mu) * lax.rsqrt(var + EPS)
    b, h, n, d = y.shape
    y = y.transpose(0, 2, 1, 3).reshape(b, n, h * d)
    return (y * jax.nn.silu(g.astype(jnp.float32))).astype(g.dtype)


def short_conv(p, conv_w):
    bg, cg, xv = p[..., 4 * MIX:5 * MIX], p[..., 5 * MIX:6 * MIX], p[..., 6 * MIX:7 * MIX]
    u = jnp.pad(cg * xv, ((0, 0), (1, 1), (0, 0)))
    conv = u[:, :-2] * conv_w[0] + u[:, 1:-1] * conv_w[1] + u[:, 2:] * conv_w[2]
    return bg * conv


def fourier_mix(u):
    b, n, _ = u.shape
    z = jnp.fft.fft2(u.astype(jnp.float32).reshape(b, n, FFT_GROUPS, FFT_GROUP_DIM), axes=(1, 3), norm='ortho')
    return jnp.real(z).reshape(b, n, MIX).astype(u.dtype)


def multiscale_pool(u, pool_w, pool_scale):
    b, n, _ = u.shape
    uf = u.astype(jnp.float32)
    csum = jnp.concatenate([jnp.zeros((b, 1, MIX), jnp.float32), jnp.cumsum(uf, axis=1)], axis=1)
    t = jnp.arange(n)
    outs = []
    for gi, w in enumerate(POOL_WINDOWS):
        lo = jnp.clip(t - w // 2, 0, n)
        hi = jnp.clip(t - w // 2 + w, 0, n)
        sl = slice(gi * POOL_GROUP, (gi + 1) * POOL_GROUP)
        cs = csum[:, :, sl]
        mean = (cs[:, hi] - cs[:, lo]) / (hi - lo).astype(jnp.float32)[None, :, None]
        outs.append(mean - uf[:, :, sl])
    pooled = jnp.stack(outs, axis=2)
    mixed = jnp.einsum('bngi,gio->bngo', pooled, pool_w.astype(jnp.float32)).reshape(b, n, MIX)
    return (mixed * pool_scale.astype(jnp.float32)).astype(u.dtype)


def mixer_output(p, ret_y, conv_w, pool_w, pool_scale, w_branch, w_o):
    branches = (retention_output(ret_y, p[..., 3 * MIX:4 * MIX]),
                short_conv(p, conv_w),
                fourier_mix(p[..., 7 * MIX:8 * MIX]),
                multiscale_pool(p[..., 8 * MIX:9 * MIX], pool_w, pool_scale))
    gates = jax.nn.sigmoid(p[..., 9 * MIX:])
    merged = 0.0
    for i, br in enumerate(branches):
        merged = merged + gates[..., i * D_MODEL:(i + 1) * D_MODEL] * (br @ w_branch[i])
    return merged @ w_o


def swiglu(h, w_up, w_down):
    a, u = jnp.split(h @ w_up, 2, axis=-1)
    return (jax.nn.silu(a) * u) @ w_down


def setup_inputs(seed: int = 0) -> dict:
    key = jax.random.key(seed)
    ks = jax.random.split(key, 16)
    nrm = lambda k, shape, s: jax.random.normal(k, shape, jnp.float32) * s
    base_decay = jnp.asarray(np.log(2.0 ** (5 + np.arange(RET_HEADS)) - 1.0).astype(np.float32))
    return {
        'x': nrm(ks[0], (BATCH, SEQ, D_MODEL), 1.0),
        'c': nrm(ks[1], (BATCH, D_MODEL), 1.0),
        'ctx': nrm(ks[2], (BATCH, CTX_LEN, D_MODEL), 1.0),
        'c_ctx': nrm(ks[3], (D_MODEL,), 1.0),
        'w_mod': nrm(ks[4], (DEPTH, D_MODEL, 6 * D_MODEL), 0.5 * D_MODEL ** -0.5),
        'b_mod': nrm(ks[5], (DEPTH, 6 * D_MODEL), 0.02),
        'norm_g': 1.0 + nrm(ks[6], (DEPTH, 4, D_MODEL), 0.02),
        'w_in': nrm(ks[7], (DEPTH, D_MODEL, IN_WIDTH), D_MODEL ** -0.5),
        'ret_decay': base_decay[None, None, :] + nrm(ks[8], (DEPTH, 2, RET_HEADS), 0.1),
        'conv_w': nrm(ks[9], (DEPTH, 3, MIX), 3 ** -0.5),
        'pool_w': nrm(ks[10], (DEPTH, len(POOL_WINDOWS), POOL_GROUP, POOL_GROUP), POOL_GROUP ** -0.5),
        'pool_scale': 1.0 + nrm(ks[11], (DEPTH, MIX), 0.02),
        'w_branch': nrm(ks[12], (DEPTH, N_BRANCH, MIX, D_MODEL), MIX ** -0.5),
        'w_o': nrm(ks[13], (DEPTH, D_MODEL, D_MODEL), D_MODEL ** -0.5),
        'ffn_w_up': nrm(ks[14], (DEPTH, D_MODEL, 2 * D_FF), D_MODEL ** -0.5),
        'ffn_w_down': nrm(ks[15], (DEPTH, D_FF, D_MODEL), D_FF ** -0.5),
    }


def reference(x, c, ctx, c_ctx, w_mod, b_mod, norm_g, w_in, ret_decay, conv_w, pool_w,
              pool_scale, w_branch, w_o, ffn_w_up, ffn_w_down):
    n = x.shape[1]
    ROWS = n // GRID_W
    rows = jnp.repeat(jnp.arange(ROWS, dtype=jnp.float32), GRID_W)
    cols = jnp.tile(jnp.arange(GRID_W, dtype=jnp.float32), ROWS)
    silu_c = jax.nn.silu(c)
    silu_cc = jax.nn.silu(c_ctx)
    scale_q = RET_HEAD_DIM ** -0.5
    for l in range(DEPTH):
        need_ctx = l < DEPTH - 1
        mod_l = (silu_c @ w_mod[l] + b_mod[l])[:, None, :]
        mod_c = (silu_cc @ w_mod[l] + b_mod[l])[None, None, :]
        sh1, sc1, g1, sh2, sc2, g2 = jnp.split(mod_l, 6, axis=-1)
        csh1, csc1, cg1, csh2, csc2, cg2 = jnp.split(mod_c, 6, axis=-1)

        h_lat = rmsnorm(x, norm_g[l, 0]) * (1.0 + sc1) + sh1
        h_ctx = rmsnorm(ctx, norm_g[l, 0]) * (1.0 + csc1) + csh1
        p_lat = h_lat @ w_in[l]
        p_ctx = h_ctx @ w_in[l]

        log_g = jax.nn.log_sigmoid(ret_decay[l].astype(jnp.float32))
        q_l = rope_2d(heads(p_lat[..., 0:MIX]) * scale_q, rows, cols)
        k_l = rope_2d(heads(p_lat[..., MIX:2 * MIX]), rows, cols)
        v_l = heads(p_lat[..., 2 * MIX:3 * MIX])
        q_c = heads(p_ctx[..., 0:MIX]) * scale_q
        k_c = heads(p_ctx[..., MIX:2 * MIX])
        v_c = heads(p_ctx[..., 2 * MIX:3 * MIX])
        s0 = jnp.zeros((x.shape[0], RET_HEADS, RET_HEAD_DIM, RET_HEAD_DIM), jnp.float32)
        y_c, s_f, s_b = bidirectional_retention(q_c, k_c, v_c, log_g, s0, s0)
        y_l, _, _ = bidirectional_retention(q_l, k_l, v_l, log_g, s_f, s_b)

        mix_lat = mixer_output(p_lat, y_l, conv_w[l], pool_w[l], pool_scale[l], w_branch[l], w_o[l])
        x = x + g1 * rmsnorm(mix_lat, norm_g[l, 1])
        h = rmsnorm(x, norm_g[l, 2]) * (1.0 + sc2) + sh2
        x = x + g2 * rmsnorm(swiglu(h, ffn_w_up[l], ffn_w_down[l]), norm_g[l, 3])

        if need_ctx:
            mix_ctx = mixer_output(p_ctx, y_c, conv_w[l], pool_w[l], pool_scale[l], w_branch[l], w_o[l])
            ctx = ctx + cg1 * rmsnorm(mix_ctx, norm_g[l, 1])
            hc = rmsnorm(ctx, norm_g[l, 2]) * (1.0 + csc2) + csh2
            ctx = ctx + cg2 * rmsnorm(swiglu(hc, ffn_w_up[l], ffn_w_down[l]), norm_g[l, 3])
    return x
```

```python
import functools

import numpy as np
import jax
import jax.numpy as jnp
from jax import lax
from jax.experimental import pallas as pl
from jax.experimental.pallas import tpu as pltpu

D_MODEL = 1024
DEPTH = 4
GRID_W = 64
MIX = 256
N_BRANCH = 4
RET_HEADS = 4
RET_HEAD_DIM = MIX // RET_HEADS
POOL_WINDOWS = (2, 4, 8, 16)
POOL_GROUP = MIX // len(POOL_WINDOWS)
FFT_GROUPS = 4
FFT_GROUP_DIM = MIX // FFT_GROUPS
D_FF = 2816
PA_WIDTH = 9 * MIX
PG_WIDTH = N_BRANCH * D_MODEL
ROPE_BASE = 10000.0
EPS = 1e-6

RET_CHUNK = 256
POOL_PAD = 8
MOD_ROWS = 16
VMEM_LIMIT = 60 * 1024 * 1024

F32 = jnp.float32
BF16 = jnp.bfloat16


def _dot(a, b):
    return jnp.dot(a, b, preferred_element_type=F32)


def _rms(xf, g):
    ms = jnp.mean(xf * xf, axis=-1, keepdims=True)
    return xf * lax.rsqrt(ms + EPS) * g


def _silu(x):
    return x * jax.nn.sigmoid(x)


def _const_spec(shape):
    zeros = (0,) * len(shape)
    return pl.BlockSpec(shape, lambda *_: zeros, pipeline_mode=pl.Buffered(1))


def _params(*sem):
    return pltpu.CompilerParams(dimension_semantics=sem, vmem_limit_bytes=VMEM_LIMIT)


def _mod_kernel(c_ref, w_ref, b_ref, o_ref):
    a = _silu(c_ref[...]).astype(BF16)
    o_ref[0] = _dot(a, w_ref[0].astype(BF16)) + b_ref[0]


def _modulation(cc, w_mod, b_mod):
    tn = 1536
    n_out = w_mod.shape[-1]
    return pl.pallas_call(
        _mod_kernel,
        out_shape=jax.ShapeDtypeStruct((DEPTH, MOD_ROWS, n_out), F32),
        grid=(DEPTH, n_out // tn),
        in_specs=[
            pl.BlockSpec((MOD_ROWS, D_MODEL), lambda l, j: (0, 0)),
            pl.BlockSpec((1, D_MODEL, tn), lambda l, j: (l, 0, j)),
            pl.BlockSpec((1, 1, tn), lambda l, j: (l, 0, j)),
        ],
        out_specs=pl.BlockSpec((1, MOD_ROWS, tn), lambda l, j: (l, 0, j)),
        compiler_params=_params("parallel", "parallel"),
        name="modulation",
    )(cc, w_mod, b_mod.reshape(DEPTH, 1, n_out))


def _in_kernel(x_ref, mod_ref, g_ref, wa_ref, wg_ref, pa_ref, pg_ref):
    h = _rms(x_ref[0], g_ref[0:1]) * (1.0 + mod_ref[0, 1:2]) + mod_ref[0, 0:1]
    hb = h.astype(BF16)
    for j in range(0, PA_WIDTH, 768):
        pa_ref[0, :, j:j + 768] = _dot(hb, wa_ref[:, j:j + 768]).astype(BF16)
    for j in range(0, PG_WIDTH, 1024):
        pg_ref[0, :, j:j + 1024] = _dot(hb, wg_ref[:, j:j + 1024]).astype(BF16)


def _in_proj(x, mod, norm_g, wa, wg, tm):
    b, n, _ = x.shape
    per_batch = mod.shape[0] > 1
    return pl.pallas_call(
        _in_kernel,
        out_shape=(jax.ShapeDtypeStruct((b, n, PA_WIDTH), BF16),
                   jax.ShapeDtypeStruct((b, n, PG_WIDTH), BF16)),
        grid=(b, n // tm),
        in_specs=[
            pl.BlockSpec((1, tm, D_MODEL), lambda i, j: (i, j, 0)),
            pl.BlockSpec((1, 6, D_MODEL), lambda i, j: (i if per_batch else 0, 0, 0)),
            _const_spec((4, D_MODEL)),
            _const_spec((D_MODEL, PA_WIDTH)),
            _const_spec((D_MODEL, PG_WIDTH)),
        ],
        out_specs=(pl.BlockSpec((1, tm, PA_WIDTH), lambda i, j: (i, j, 0)),
                   pl.BlockSpec((1, tm, PG_WIDTH), lambda i, j: (i, j, 0))),
        compiler_params=_params("parallel", "parallel"),
        name="in_proj",
    )(x, mod, norm_g, wa, wg)


def _ret_kernel(q_ref, k_ref, v_ref, g_ref, cos_ref, sin_ref, rd_ref, s0_ref, avg_ref,
                o_ref, sout_ref, qr_s, kr_s, sf_s, sb_s, dm_s, dec_s, *, n):
    c_len = RET_CHUNK
    nc = n // c_len
    lane = lax.broadcasted_iota(jnp.int32, (c_len, MIX), 1)
    head = lane // RET_HEAD_DIM

    @pl.when(pl.program_id(0) == 0)
    def _decay_tables():
        rd = rd_ref[...]
        lg = jnp.minimum(rd, 0.0) - jnp.log1p(jnp.exp(-jnp.abs(rd)))
        lgf, lgb = lg[0:1], lg[1:2]
        idx = lax.broadcasted_iota(jnp.int32, (c_len, MIX), 0).astype(F32)
        dec_s[0] = jnp.exp(lgf * (idx + 1.0))
        dec_s[1] = jnp.exp(lgf * (c_len - 1.0 - idx))
        dec_s[2] = jnp.exp(lgb * (c_len - idx))
        dec_s[3] = jnp.exp(lgb * idx)
        dec_s[4] = jnp.exp(lgf * float(c_len)) + 0.0 * idx
        dec_s[5] = jnp.exp(lgb * float(c_len)) + 0.0 * idx
        ii = lax.broadcasted_iota(jnp.int32, (c_len, c_len), 0)
        jj = lax.broadcasted_iota(jnp.int32, (c_len, c_len), 1)
        diff = (ii - jj).astype(F32)
        for h in range(RET_HEADS):
            lo = h * RET_HEAD_DIM
            rate = jnp.where(diff >= 0.0, lgf[:, lo:lo + 1], -lgb[:, lo:lo + 1])
            dm_s[h] = jnp.exp(rate * diff)

    first_half = (lane & 16) == 0
    for c in range(nc):
        rows = pl.ds(c * c_len, c_len)
        cos, sin = cos_ref[rows, :], sin_ref[rows, :]
        for src, dst, scale in ((q_ref, qr_s, RET_HEAD_DIM ** -0.5), (k_ref, kr_s, 1.0)):
            t = src[0, rows, :].astype(F32)
            partner = jnp.where(first_half, pltpu.roll(t, MIX - 16, 1), pltpu.roll(t, 16, 1))
            dst[rows, :] = ((t * cos + partner * sin) * scale).astype(BF16)

    row_head = lax.broadcasted_iota(jnp.int32, (MIX, MIX), 0) // RET_HEAD_DIM
    col_head = lax.broadcasted_iota(jnp.int32, (MIX, MIX), 1) // RET_HEAD_DIM
    block_diag = row_head == col_head
    tn_dims = (((0,), (0,)), ((), ()))

    state = s0_ref[0, 0]
    for c in range(nc):
        rows = pl.ds(c * c_len, c_len)
        sf_s[c] = state.astype(BF16)
        kd = (kr_s[rows, :].astype(F32) * dec_s[1]).astype(BF16)
        upd = lax.dot_general(kd, v_ref[0, rows, :], tn_dims, preferred_element_type=F32)
        state = state * dec_s[4] + jnp.where(block_diag, upd, 0.0)
    sout_ref[0, 0] = state
    state = s0_ref[0, 1]
    for c in range(nc - 1, -1, -1):
        rows = pl.ds(c * c_len, c_len)
        sb_s[c] = state.astype(BF16)
        kd = (kr_s[rows, :].astype(F32) * dec_s[3]).astype(BF16)
        upd = lax.dot_general(kd, v_ref[0, rows, :], tn_dims, preferred_element_type=F32)
        state = state * dec_s[5] + jnp.where(block_diag, upd, 0.0)
    sout_ref[0, 1] = state

    nt_dims = (((1,), (1,)), ((), ()))
    for c in range(nc):
        rows = pl.ds(c * c_len, c_len)
        q = qr_s[rows, :]
        k = kr_s[rows, :]
        v = v_ref[0, rows, :]
        qf = q.astype(F32)
        y = (_dot((qf * dec_s[0]).astype(BF16), sf_s[c])
             + _dot((qf * dec_s[2]).astype(BF16), sb_s[c]))
        for h in range(RET_HEADS):
            in_head = head == h
            s = lax.dot_general(jnp.where(in_head, q, jnp.zeros_like(q)), k, nt_dims,
                                preferred_element_type=F32)
            o = _dot((s * dm_s[h]).astype(BF16), v)
            y = y + jnp.where(in_head, o, 0.0)
        avg = avg_ref[...]

        def head_mean(t):
            hi = t.astype(BF16)
            lo = (t - hi.astype(F32)).astype(BF16)
            return _dot(hi, avg) + _dot(lo, avg)

        d = y - head_mean(y)
        yn = d * lax.rsqrt(head_mean(d * d) + EPS)
        o_ref[0, rows, :] = (yn * _silu(g_ref[0, rows, :].astype(F32))).astype(BF16)


def _retention(pa, cos, sin, rd_lane, s0, avg):
    b, n, _ = pa.shape
    nc = n // RET_CHUNK
    col = lambda j: pl.BlockSpec((1, n, MIX), lambda i: (i, 0, j))
    return pl.pallas_call(
        functools.partial(_ret_kernel, n=n),
        out_shape=(jax.ShapeDtypeStruct((b, n, MIX), BF16),
                   jax.ShapeDtypeStruct((b, 2, MIX, MIX), F32)),
        grid=(b,),
        in_specs=[col(0), col(1), col(2), col(3),
                  _const_spec((n, MIX)), _const_spec((n, MIX)),
                  _const_spec((2, MIX)),
                  pl.BlockSpec((1, 2, MIX, MIX), lambda i: (i, 0, 0, 0)),
                  _const_spec((MIX, MIX))],
        out_specs=(pl.BlockSpec((1, n, MIX), lambda i: (i, 0, 0)),
                   pl.BlockSpec((1, 2, MIX, MIX), lambda i: (i, 0, 0, 0))),
        scratch_shapes=[pltpu.VMEM((n, MIX), BF16), pltpu.VMEM((n, MIX), BF16),
                        pltpu.VMEM((nc, MIX, MIX), BF16), pltpu.VMEM((nc, MIX, MIX), BF16),
                        pltpu.VMEM((RET_HEADS, RET_CHUNK, RET_CHUNK), F32),
                        pltpu.VMEM((6, RET_CHUNK, MIX), F32)],
        compiler_params=_params("arbitrary"),
        name="retention",
    )(pa, pa, pa, pa, cos, sin, rd_lane, s0, avg)


def _fourier_kernel(u_ref, cc_ref, ss_ref, dft_ref, o_ref, *, n, scale):
    u = u_ref[0]
    uc = jnp.concatenate([_dot(u, cc_ref[...]).astype(BF16),
                          _dot(u, ss_ref[...]).astype(BF16)], axis=0)
    tr = min(n, 512)
    for r in range(0, n, tr):
        o_ref[0, r:r + tr, :] = (_dot(dft_ref[r:r + tr, :], uc) * scale).astype(BF16)


def _fourier(pa, cc, ss, dft):
    b, n, _ = pa.shape
    scale = float(1.0 / np.sqrt(n * FFT_GROUP_DIM))
    return pl.pallas_call(
        functools.partial(_fourier_kernel, n=n, scale=scale),
        out_shape=jax.ShapeDtypeStruct((b, n, MIX), BF16),
        grid=(b,),
        in_specs=[pl.BlockSpec((1, n, MIX), lambda i: (i, 0, 7)),
                  _const_spec((MIX, MIX)), _const_spec((MIX, MIX)),
                  _const_spec((n, 2 * n))],
        out_specs=pl.BlockSpec((1, n, MIX), lambda i: (i, 0, 0)),
        compiler_params=_params("parallel"),
        name="fourier",
    )(pa, cc, ss, dft)


def _conv_pool_kernel(cb_ref, cg_ref, cx_ref, pu_ref, cw_ref, pw_ref, ps_ref,
                      conv_ref, pool_ref, ext_s, *, n):
    row = lax.broadcasted_iota(jnp.int32, (n, MIX), 0)
    u = cg_ref[0].astype(F32) * cx_ref[0].astype(F32)
    prev = jnp.where(row == 0, 0.0, pltpu.roll(u, 1, 0))
    nxt = jnp.where(row == n - 1, 0.0, pltpu.roll(u, n - 1, 0))
    conv = prev * cw_ref[0:1] + u * cw_ref[1:2] + nxt * cw_ref[2:3]
    conv_ref[0] = (cb_ref[0].astype(F32) * conv).astype(BF16)

    uf = pu_ref[0].astype(F32)
    ne = n + 2 * POOL_PAD
    ext_s[0:POOL_PAD, :] = jnp.zeros((POOL_PAD, MIX), F32)
    ext_s[POOL_PAD + n:ne, :] = jnp.zeros((POOL_PAD, MIX), F32)
    ext_s[POOL_PAD:POOL_PAD + n, :] = uf
    acc = ext_s[...]
    group = lax.broadcasted_iota(jnp.int32, (n, MIX), 1) // POOL_GROUP
    rowf = row.astype(F32)
    mean = jnp.zeros((n, MIX), F32)
    for gi, w in enumerate(POOL_WINDOWS):
        acc = acc + pltpu.roll(acc, w // 2, 0)
        lead = w // 2 - 1
        win = (pltpu.roll(acc, ne - lead, 0) if lead else acc)[POOL_PAD:POOL_PAD + n]
        cnt = jnp.minimum(rowf + (w - w // 2), float(n)) - jnp.maximum(rowf - w // 2, 0.0)
        mean = jnp.where(group == gi, win / cnt, mean)
    pooled = (mean - uf).astype(BF16)
    pool_ref[0] = (_dot(pooled, pw_ref[...]) * ps_ref[...]).astype(BF16)


def _conv_pool(pa, conv_w, pool_bd, pool_scale):
    b, n, _ = pa.shape
    col = lambda j: pl.BlockSpec((1, n, MIX), lambda i: (i, 0, j))
    out = pl.BlockSpec((1, n, MIX), lambda i: (i, 0, 0))
    return pl.pallas_call(
        functools.partial(_conv_pool_kernel, n=n),
        out_shape=(jax.ShapeDtypeStruct((b, n, MIX), BF16),
                   jax.ShapeDtypeStruct((b, n, MIX), BF16)),
        grid=(b,),
        in_specs=[col(4), col(5), col(6), col(8),
                  _const_spec((3, MIX)), _const_spec((MIX, MIX)), _const_spec((1, MIX))],
        out_specs=(out, out),
        scratch_shapes=[pltpu.VMEM((n + 2 * POOL_PAD, MIX), F32)],
        compiler_params=_params("parallel"),
        name="conv_pool",
    )(pa, pa, pa, pa, conv_w, pool_bd, pool_scale)


FFN_SPLITS = (0, 1536, D_FF)


def _post_kernel(b0_ref, b1_ref, b2_ref, b3_ref, pg_ref, x_ref, mod_ref, g_ref,
                 wb_ref, wo_ref, wup_ref, wdn_ref, o_ref):
    merged = None
    for i, br in enumerate((b0_ref, b1_ref, b2_ref, b3_ref)):
        gate = jax.nn.sigmoid(pg_ref[0, :, i * D_MODEL:(i + 1) * D_MODEL].astype(F32))
        t = gate * _dot(br[0], wb_ref[i])
        merged = t if merged is None else merged + t
    mix = _dot(merged.astype(BF16), wo_ref[...])
    x1 = x_ref[0] + mod_ref[0, 2:3] * _rms(mix, g_ref[1:2])

    h = (_rms(x1, g_ref[2:3]) * (1.0 + mod_ref[0, 4:5]) + mod_ref[0, 3:4]).astype(BF16)
    ffn = None
    for lo, hi in zip(FFN_SPLITS[:-1], FFN_SPLITS[1:]):
        a = _dot(h, wup_ref[:, lo:hi])
        u = _dot(h, wup_ref[:, D_FF + lo:D_FF + hi])
        t = _dot((_silu(a) * u).astype(BF16), wdn_ref[lo:hi, :])
        ffn = t if ffn is None else ffn + t
    o_ref[0] = x1 + mod_ref[0, 5:6] * _rms(ffn, g_ref[3:4])


def _post(branches, pg, x, mod, norm_g, wb, wo, wup, wdn, tm):
    b, n, _ = x.shape
    per_batch = mod.shape[0] > 1
    br = pl.BlockSpec((1, tm, MIX), lambda i, j: (i, j, 0))
    return pl.pallas_call(
        _post_kernel,
        out_shape=jax.ShapeDtypeStruct((b, n, D_MODEL), F32),
        grid=(b, n // tm),
        in_specs=[br, br, br, br,
                  pl.BlockSpec((1, tm, PG_WIDTH), lambda i, j: (i, j, 0)),
                  pl.BlockSpec((1, tm, D_MODEL), lambda i, j: (i, j, 0)),
                  pl.BlockSpec((1, 6, D_MODEL), lambda i, j: (i if per_batch else 0, 0, 0)),
                  _const_spec((4, D_MODEL)),
                  _const_spec((N_BRANCH, MIX, D_MODEL)),
                  _const_spec((D_MODEL, D_MODEL)),
                  _const_spec((D_MODEL, 2 * D_FF)),
                  _const_spec((D_FF, D_MODEL))],
        out_specs=pl.BlockSpec((1, tm, D_MODEL), lambda i, j: (i, j, 0)),
        compiler_params=_params("parallel", "parallel"),
        name="merge_ffn",
    )(*branches, pg, x, mod, norm_g, wb, wo, wup, wdn)


def _rope_tables(n):
    rows = np.repeat(np.arange(n // GRID_W, dtype=np.float64), GRID_W)
    cols = np.tile(np.arange(GRID_W, dtype=np.float64), n // GRID_W)
    lane = np.arange(MIX)
    in_head = lane % RET_HEAD_DIM
    pos = np.where((in_head < RET_HEAD_DIM // 2)[None, :], rows[:, None], cols[:, None])
    quarter = RET_HEAD_DIM // 4
    freq = ROPE_BASE ** (-np.arange(quarter, dtype=np.float64) / quarter)
    ang = pos * freq[lane % quarter][None, :]
    sign = np.where((lane % (2 * quarter)) < quarter, -1.0, 1.0)[None, :]
    return np.cos(ang).astype(np.float32), (np.sin(ang) * sign).astype(np.float32)


def _dft_tables(n):
    k = np.arange(n, dtype=np.int64)
    ang = 2.0 * np.pi * ((k[:, None] * k[None, :]) % n) / n
    dft = np.concatenate([np.cos(ang), -np.sin(ang)], axis=1).astype(np.float32)
    m = np.arange(FFT_GROUP_DIM, dtype=np.int64)
    angc = 2.0 * np.pi * ((m[:, None] * m[None, :]) % FFT_GROUP_DIM) / FFT_GROUP_DIM
    eye = np.eye(FFT_GROUPS)
    cc = np.kron(eye, np.cos(angc)).astype(np.float32)
    ss = np.kron(eye, np.sin(angc)).astype(np.float32)
    return dft, cc, ss


def _block_diag(w):
    g, d, _ = w.shape
    eye = jnp.eye(g, dtype=w.dtype)
    return (eye[:, None, :, None] * w[:, :, None, :]).reshape(g * d, g * d)


def kernel(x, c, ctx, c_ctx, w_mod, b_mod, norm_g, w_in, ret_decay, conv_w, pool_w, pool_scale,
           w_branch, w_o, ffn_w_up, ffn_w_down):
    b, n, _ = x.shape
    n_ctx = ctx.shape[1]

    cc_rows = jnp.concatenate(
        [c, c_ctx[None, :], jnp.zeros((MOD_ROWS - b - 1, D_MODEL), F32)], axis=0)
    mod_all = _modulation(cc_rows, w_mod, b_mod)

    cos_l, sin_l = (jnp.asarray(t) for t in _rope_tables(n))
    cos_c, sin_c = jnp.ones((n_ctx, MIX), F32), jnp.zeros((n_ctx, MIX), F32)
    dft_l, cc_t, ss_t = (jnp.asarray(t).astype(BF16) for t in _dft_tables(n))
    dft_c = jnp.asarray(_dft_tables(n_ctx)[0]).astype(BF16)
    avg = jnp.asarray(np.kron(np.eye(RET_HEADS), np.full((RET_HEAD_DIM, RET_HEAD_DIM),
                                                         1.0 / RET_HEAD_DIM)), F32).astype(BF16)
    s_zero = jnp.zeros((b, 2, MIX, MIX), F32)

    for l in range(DEPTH):
        need_ctx = l < DEPTH - 1
        mod_lat = mod_all[l, :b].reshape(b, 6, D_MODEL)
        mod_ctx = mod_all[l, b:b + 1].reshape(1, 6, D_MODEL)
        wa = w_in[l, :, :PA_WIDTH].astype(BF16)
        wg = w_in[l, :, PA_WIDTH:].astype(BF16)
        wb = w_branch[l].astype(BF16)
        wo = w_o[l].astype(BF16)
        wup = ffn_w_up[l].astype(BF16)
        wdn = ffn_w_down[l].astype(BF16)
        pool_bd = _block_diag(pool_w[l]).astype(BF16)
        rd_lane = jnp.repeat(ret_decay[l].astype(F32), RET_HEAD_DIM, axis=1)
        ps = pool_scale[l].reshape(1, MIX)

        pa_c, pg_c = _in_proj(ctx, mod_ctx, norm_g[l], wa, wg, tm=n_ctx)
        pa_l, pg_l = _in_proj(x, mod_lat, norm_g[l], wa, wg, tm=512)

        ret_c, s_ctx = _retention(pa_c, cos_c, sin_c, rd_lane, s_zero, avg)
        ret_l, _ = _retention(pa_l, cos_l, sin_l, rd_lane, s_ctx, avg)

        four_l = _fourier(pa_l, cc_t, ss_t, dft_l)
        conv_l, pool_l = _conv_pool(pa_l, conv_w[l], pool_bd, ps)
        x = _post((ret_l, conv_l, four_l, pool_l), pg_l, x, mod_lat, norm_g[l],
                  wb, wo, wup, wdn, tm=512)

        if need_ctx:
            four_c = _fourier(pa_c, cc_t, ss_t, dft_c)
            conv_c, pool_c = _conv_pool(pa_c, conv_w[l], pool_bd, ps)
            ctx = _post((ret_c, conv_c, four_c, pool_c), pg_c, ctx, mod_ctx, norm_g[l],
                        wb, wo, wup, wdn, tm=n_ctx)
    return x
```

```python
import functools

import numpy as np
import jax
import jax.numpy as jnp
from jax import lax
from jax.experimental import pallas as pl
from jax.experimental.pallas import tpu as pltpu

D_MODEL = 1024
DEPTH = 4
GRID_W = 64
MIX = 256
N_BRANCH = 4
RET_HEADS = 4
RET_HEAD_DIM = MIX // RET_HEADS
POOL_WINDOWS = (2, 4, 8, 16)
POOL_GROUP = MIX // len(POOL_WINDOWS)
FFT_GROUPS = 4
FFT_GROUP_DIM = MIX // FFT_GROUPS
D_FF = 2816
PA_WIDTH = 9 * MIX
PG_WIDTH = N_BRANCH * D_MODEL
ROPE_BASE = 10000.0
EPS = 1e-6

LANES = 128
RET_CHUNK = 256
SEQ_PAD = 8
MOD_ROWS = 16
VMEM_LIMIT = 60 * 1024 * 1024

F32 = jnp.float32
BF16 = jnp.bfloat16


def _dot(a, b):
    return jnp.dot(a, b, preferred_element_type=F32)


def _rms(xf, g):
    ms = jnp.mean(xf * xf, axis=-1, keepdims=True)
    return xf * lax.rsqrt(ms + EPS) * g


def _silu(x):
    return x * jax.nn.sigmoid(x)


def _const_spec(shape):
    zeros = (0,) * len(shape)
    return pl.BlockSpec(shape, lambda *_: zeros, pipeline_mode=pl.Buffered(1))


def _layer_spec(shape, layer):
    tail = (0,) * (len(shape) - 1)
    return pl.BlockSpec((1,) + tuple(shape[1:]), lambda *_: (layer,) + tail,
                        pipeline_mode=pl.Buffered(1))


def _mod_spec(layer, row):
    return pl.BlockSpec((1, 1, 6, D_MODEL),
                        lambda i, j: (layer, i if row is None else row, 0, 0))


def _params(*sem):
    return pltpu.CompilerParams(dimension_semantics=sem, vmem_limit_bytes=VMEM_LIMIT)


def _mod_kernel(c_ref, w_ref, b_ref, o_ref):
    a = _silu(c_ref[...]).astype(BF16)
    o_ref[0] = _dot(a, w_ref[0].astype(BF16)) + b_ref[0]


def _modulation(cc, w_mod, b_mod):
    tn = 1536
    n_out = w_mod.shape[-1]
    return pl.pallas_call(
        _mod_kernel,
        out_shape=jax.ShapeDtypeStruct((DEPTH, MOD_ROWS, n_out), F32),
        grid=(DEPTH, n_out // tn),
        in_specs=[
            pl.BlockSpec((MOD_ROWS, D_MODEL), lambda l, j: (0, 0)),
            pl.BlockSpec((1, D_MODEL, tn), lambda l, j: (l, 0, j)),
            pl.BlockSpec((1, 1, tn), lambda l, j: (l, 0, j)),
        ],
        out_specs=pl.BlockSpec((1, MOD_ROWS, tn), lambda l, j: (l, 0, j)),
        compiler_params=_params("parallel", "parallel"),
        name="modulation",
    )(cc, w_mod, b_mod.reshape(DEPTH, 1, n_out))


def _in_kernel(x_ref, mod_ref, g_ref, w_ref, pa_ref, pg_ref):
    mod = mod_ref[0, 0]
    h = _rms(x_ref[0], g_ref[0, 0:1] * (1.0 + mod[1:2])) + mod[0:1]
    hb = h.astype(BF16)
    for j in range(0, PA_WIDTH, 768):
        pa_ref[0, :, j:j + 768] = _dot(hb, w_ref[0, :, j:j + 768]).astype(BF16)
    for j in range(0, PG_WIDTH, 1024):
        gate = jax.nn.sigmoid(_dot(hb, w_ref[0, :, PA_WIDTH + j:PA_WIDTH + j + 1024]))
        pg_ref[0, :, j:j + 1024] = gate.astype(BF16)


def _in_proj(x, mod, norm_g, w_in, layer, mod_row, tm):
    b, n, _ = x.shape
    return pl.pallas_call(
        _in_kernel,
        out_shape=(jax.ShapeDtypeStruct((b, n, PA_WIDTH), BF16),
                   jax.ShapeDtypeStruct((b, n, PG_WIDTH), BF16)),
        grid=(b, n // tm),
        in_specs=[
            pl.BlockSpec((1, tm, D_MODEL), lambda i, j: (i, j, 0)),
            _mod_spec(layer, mod_row),
            _layer_spec(norm_g.shape, layer),
            _layer_spec(w_in.shape, layer),
        ],
        out_specs=(pl.BlockSpec((1, tm, PA_WIDTH), lambda i, j: (i, j, 0)),
                   pl.BlockSpec((1, tm, PG_WIDTH), lambda i, j: (i, j, 0))),
        compiler_params=_params("parallel", "parallel"),
        name="in_proj",
    )(x, mod, norm_g, w_in)


def _ret_kernel(q_ref, k_ref, v_ref, g_ref, cos_ref, sin_ref, rd_ref, s0_ref, avg_ref,
                o_ref, sout_ref, qr_s, kr_s, sf_s, sb_s, dm_s, dec_s, *, n):
    c_len = RET_CHUNK
    nc = n // c_len
    lane = lax.broadcasted_iota(jnp.int32, (c_len, MIX), 1)
    head = lane // RET_HEAD_DIM

    @pl.when(pl.program_id(0) == 0)
    def _decay_tables():
        rd = rd_ref[0]
        lg = jnp.minimum(rd, 0.0) - jnp.log1p(jnp.exp(-jnp.abs(rd)))
        lgf, lgb = lg[0:1], lg[1:2]
        idx = lax.broadcasted_iota(jnp.int32, (c_len, MIX), 0).astype(F32)
        dec_s[0] = jnp.exp(lgf * (idx + 1.0))
        dec_s[1] = jnp.exp(lgf * (c_len - 1.0 - idx))
        dec_s[2] = jnp.exp(lgb * (c_len - idx))
        dec_s[3] = jnp.exp(lgb * idx)
        dec_s[4] = jnp.exp(lgf * float(c_len)) + 0.0 * idx
        dec_s[5] = jnp.exp(lgb * float(c_len)) + 0.0 * idx
        ii = lax.broadcasted_iota(jnp.int32, (c_len, c_len), 0)
        jj = lax.broadcasted_iota(jnp.int32, (c_len, c_len), 1)
        diff = (ii - jj).astype(F32)
        for h in range(RET_HEADS):
            lo = h * RET_HEAD_DIM
            rate = jnp.where(diff >= 0.0, lgf[:, lo:lo + 1], -lgb[:, lo:lo + 1])
            dm_s[h] = jnp.exp(rate * diff)

    first_half = (lane & 16) == 0
    for c in range(nc):
        rows = pl.ds(c * c_len, c_len)
        cos, sin = cos_ref[rows, :], sin_ref[rows, :]
        for src, dst, scale in ((q_ref, qr_s, RET_HEAD_DIM ** -0.5), (k_ref, kr_s, 1.0)):
            t = src[0, rows, :].astype(F32)
            partner = jnp.where(first_half, pltpu.roll(t, MIX - 16, 1), pltpu.roll(t, 16, 1))
            dst[rows, :] = ((t * cos + partner * sin) * scale).astype(BF16)

    row_head = lax.broadcasted_iota(jnp.int32, (MIX, MIX), 0) // RET_HEAD_DIM
    col_head = lax.broadcasted_iota(jnp.int32, (MIX, MIX), 1) // RET_HEAD_DIM
    block_diag = row_head == col_head
    tn_dims = (((0,), (0,)), ((), ()))

    state = s0_ref[0, 0]
    for c in range(nc):
        rows = pl.ds(c * c_len, c_len)
        sf_s[c] = state.astype(BF16)
        kd = (kr_s[rows, :].astype(F32) * dec_s[1]).astype(BF16)
        upd = lax.dot_general(kd, v_ref[0, rows, :], tn_dims, preferred_element_type=F32)
        state = state * dec_s[4] + jnp.where(block_diag, upd, 0.0)
    sout_ref[0, 0] = state
    state = s0_ref[0, 1]
    for c in range(nc - 1, -1, -1):
        rows = pl.ds(c * c_len, c_len)
        sb_s[c] = state.astype(BF16)
        kd = (kr_s[rows, :].astype(F32) * dec_s[3]).astype(BF16)
        upd = lax.dot_general(kd, v_ref[0, rows, :], tn_dims, preferred_element_type=F32)
        state = state * dec_s[5] + jnp.where(block_diag, upd, 0.0)
    sout_ref[0, 1] = state

    nt_dims = (((1,), (1,)), ((), ()))
    for c in range(nc):
        rows = pl.ds(c * c_len, c_len)
        q = qr_s[rows, :]
        k = kr_s[rows, :]
        v = v_ref[0, rows, :]
        qf = q.astype(F32)
        y = (_dot((qf * dec_s[0]).astype(BF16), sf_s[c])
             + _dot((qf * dec_s[2]).astype(BF16), sb_s[c]))
        for h in range(RET_HEADS):
            in_head = head == h
            s = lax.dot_general(jnp.where(in_head, q, jnp.zeros_like(q)), k, nt_dims,
                                preferred_element_type=F32)
            o = _dot((s * dm_s[h]).astype(BF16), v)
            y = y + jnp.where(in_head, o, 0.0)
        avg = avg_ref[...]

        def head_mean(t):
            return _dot(t.astype(BF16), avg)

        d = y - head_mean(y)
        yn = d * lax.rsqrt(head_mean(d * d) + EPS)
        o_ref[0, rows, :] = (yn * _silu(g_ref[0, rows, :].astype(F32))).astype(BF16)


def _retention(pa, cos, sin, rd_lane, s0, avg, layer):
    b, n, _ = pa.shape
    nc = n // RET_CHUNK
    col = lambda j: pl.BlockSpec((1, n, MIX), lambda i: (i, 0, j))
    return pl.pallas_call(
        functools.partial(_ret_kernel, n=n),
        out_shape=(jax.ShapeDtypeStruct((b, n, MIX), BF16),
                   jax.ShapeDtypeStruct((b, 2, MIX, MIX), F32)),
        grid=(b,),
        in_specs=[col(0), col(1), col(2), col(3),
                  _const_spec((n, MIX)), _const_spec((n, MIX)),
                  _layer_spec(rd_lane.shape, layer),
                  pl.BlockSpec((1, 2, MIX, MIX), lambda i: (i, 0, 0, 0)),
                  _const_spec((MIX, MIX))],
        out_specs=(pl.BlockSpec((1, n, MIX), lambda i: (i, 0, 0)),
                   pl.BlockSpec((1, 2, MIX, MIX), lambda i: (i, 0, 0, 0))),
        scratch_shapes=[pltpu.VMEM((n, MIX), BF16), pltpu.VMEM((n, MIX), BF16),
                        pltpu.VMEM((nc, MIX, MIX), BF16), pltpu.VMEM((nc, MIX, MIX), BF16),
                        pltpu.VMEM((RET_HEADS, RET_CHUNK, RET_CHUNK), F32),
                        pltpu.VMEM((6, RET_CHUNK, MIX), F32)],
        compiler_params=_params("arbitrary"),
        name="retention",
    )(pa, pa, pa, pa, cos, sin, rd_lane, s0, avg)


def _fourier_kernel(u_ref, cc_ref, ss_ref, dft_ref, o_ref, *, n, scale):
    u = u_ref[0]
    uc = jnp.concatenate([_dot(u, cc_ref[...]).astype(BF16),
                          _dot(u, ss_ref[...]).astype(BF16)], axis=0)
    tr = min(n, 512)
    for r in range(0, n, tr):
        o_ref[0, r:r + tr, :] = (_dot(dft_ref[r:r + tr, :], uc) * scale).astype(BF16)


def _fourier(pa, cc, ss, dft):
    b, n, _ = pa.shape
    scale = float(1.0 / np.sqrt(n * FFT_GROUP_DIM))
    return pl.pallas_call(
        functools.partial(_fourier_kernel, n=n, scale=scale),
        out_shape=jax.ShapeDtypeStruct((b, n, MIX), BF16),
        grid=(b,),
        in_specs=[pl.BlockSpec((1, n, MIX), lambda i: (i, 0, 7)),
                  _const_spec((MIX, MIX)), _const_spec((MIX, MIX)),
                  _const_spec((n, 2 * n))],
        out_specs=pl.BlockSpec((1, n, MIX), lambda i: (i, 0, 0)),
        compiler_params=_params("parallel"),
        name="fourier",
    )(pa, cc, ss, dft)


def _pad_rows(t):
    zeros = jnp.zeros((SEQ_PAD, t.shape[1]), t.dtype)
    return jnp.concatenate([zeros, t, zeros], axis=0)


def _shift_rows(t, s):
    return pltpu.roll(t, s % t.shape[0], 0)


def _conv_pool_kernel(cb_ref, cg_ref, cx_ref, pu_ref, cw_ref, pw_ref, ps_ref, icnt_ref,
                      conv_ref, pool_ref, *, n):
    main = slice(SEQ_PAD, SEQ_PAD + n)
    u = _pad_rows(cg_ref[0].astype(F32) * cx_ref[0].astype(F32))
    cw = cw_ref[0]
    conv = _shift_rows(u, 1) * cw[0:1] + u * cw[1:2] + _shift_rows(u, -1) * cw[2:3]
    conv_ref[0] = (cb_ref[0].astype(F32) * conv[main]).astype(BF16)

    uf = pu_ref[0].astype(F32)
    in_first_group = lax.broadcasted_iota(jnp.int32, (n, LANES), 1) < POOL_GROUP
    halves = []
    for half in range(MIX // LANES):
        e = _pad_rows(uf[:, half * LANES:(half + 1) * LANES])
        c = e + _shift_rows(e, 1)
        sums = [c]
        for w in POOL_WINDOWS[1:2 * half + 2]:
            c = _shift_rows(c, w // 4) + _shift_rows(c, -(w // 4))
            sums.append(c)
        halves.append(jnp.where(in_first_group, sums[-2][main], sums[-1][main]))
    win = jnp.concatenate(halves, axis=1)
    pooled = (win * icnt_ref[...] - uf).astype(BF16)
    pool_ref[0] = (_dot(pooled, pw_ref[0]) * ps_ref[0]).astype(BF16)


def _conv_pool(pa, conv_w, pool_bd, pool_scale, inv_cnt, layer):
    b, n, _ = pa.shape
    col = lambda j: pl.BlockSpec((1, n, MIX), lambda i: (i, 0, j))
    out = pl.BlockSpec((1, n, MIX), lambda i: (i, 0, 0))
    return pl.pallas_call(
        functools.partial(_conv_pool_kernel, n=n),
        out_shape=(jax.ShapeDtypeStruct((b, n, MIX), BF16),
                   jax.ShapeDtypeStruct((b, n, MIX), BF16)),
        grid=(b,),
        in_specs=[col(4), col(5), col(6), col(8),
                  _layer_spec(conv_w.shape, layer), _layer_spec(pool_bd.shape, layer),
                  _layer_spec(pool_scale.shape, layer), _const_spec((n, MIX))],
        out_specs=(out, out),
        compiler_params=_params("parallel"),
        name="conv_pool",
    )(pa, pa, pa, pa, conv_w, pool_bd, pool_scale, inv_cnt)


FFN_SPLITS = (0, 1536, D_FF)


def _post_kernel(b0_ref, b1_ref, b2_ref, b3_ref, pg_ref, x_ref, mod_ref, g_ref,
                 wb_ref, wo_ref, wup_ref, wdn_ref, o_ref):
    mod = mod_ref[0, 0]
    g = g_ref[0]
    merged = None
    for i, br in enumerate((b0_ref, b1_ref, b2_ref, b3_ref)):
        gate = pg_ref[0, :, i * D_MODEL:(i + 1) * D_MODEL].astype(F32)
        t = gate * _dot(br[0], wb_ref[0, i])
        merged = t if merged is None else merged + t
    mix = _dot(merged.astype(BF16), wo_ref[0])
    x1 = x_ref[0] + _rms(mix, g[1:2] * mod[2:3])

    h = (_rms(x1, g[2:3] * (1.0 + mod[4:5])) + mod[3:4]).astype(BF16)
    ffn = None
    for lo, hi in zip(FFN_SPLITS[:-1], FFN_SPLITS[1:]):
        a = _dot(h, wup_ref[0, :, lo:hi])
        u = _dot(h, wup_ref[0, :, D_FF + lo:D_FF + hi])
        t = _dot((_silu(a) * u).astype(BF16), wdn_ref[0, lo:hi, :])
        ffn = t if ffn is None else ffn + t
    o_ref[0] = x1 + _rms(ffn, g[3:4] * mod[5:6])


def _post(branches, pg, x, mod, norm_g, wb, wo, wup, wdn, layer, mod_row, tm):
    b, n, _ = x.shape
    br = pl.BlockSpec((1, tm, MIX), lambda i, j: (i, j, 0))
    return pl.pallas_call(
        _post_kernel,
        out_shape=jax.ShapeDtypeStruct((b, n, D_MODEL), F32),
        grid=(b, n // tm),
        in_specs=[br, br, br, br,
                  pl.BlockSpec((1, tm, PG_WIDTH), lambda i, j: (i, j, 0)),
                  pl.BlockSpec((1, tm, D_MODEL), lambda i, j: (i, j, 0)),
                  _mod_spec(layer, mod_row),
                  _layer_spec(norm_g.shape, layer),
                  _layer_spec(wb.shape, layer),
                  _layer_spec(wo.shape, layer),
                  _layer_spec(wup.shape, layer),
                  _layer_spec(wdn.shape, layer)],
        out_specs=pl.BlockSpec((1, tm, D_MODEL), lambda i, j: (i, j, 0)),
        compiler_params=_params("parallel", "parallel"),
        name="merge_ffn",
    )(*branches, pg, x, mod, norm_g, wb, wo, wup, wdn)


def _rope_tables(n):
    rows = np.repeat(np.arange(n // GRID_W, dtype=np.float64), GRID_W)
    cols = np.tile(np.arange(GRID_W, dtype=np.float64), n // GRID_W)
    lane = np.arange(MIX)
    in_head = lane % RET_HEAD_DIM
    pos = np.where((in_head < RET_HEAD_DIM // 2)[None, :], rows[:, None], cols[:, None])
    quarter = RET_HEAD_DIM // 4
    freq = ROPE_BASE ** (-np.arange(quarter, dtype=np.float64) / quarter)
    ang = pos * freq[lane % quarter][None, :]
    sign = np.where((lane % (2 * quarter)) < quarter, -1.0, 1.0)[None, :]
    return np.cos(ang).astype(np.float32), (np.sin(ang) * sign).astype(np.float32)


def _dft_tables(n):
    k = np.arange(n, dtype=np.int64)
    ang = 2.0 * np.pi * ((k[:, None] * k[None, :]) % n) / n
    dft = np.concatenate([np.cos(ang), -np.sin(ang)], axis=1).astype(np.float32)
    m = np.arange(FFT_GROUP_DIM, dtype=np.int64)
    angc = 2.0 * np.pi * ((m[:, None] * m[None, :]) % FFT_GROUP_DIM) / FFT_GROUP_DIM
    eye = np.eye(FFT_GROUPS)
    cc = np.kron(eye, np.cos(angc)).astype(np.float32)
    ss = np.kron(eye, np.sin(angc)).astype(np.float32)
    return dft, cc, ss


def _pool_inv_counts(n):
    t = np.arange(n)
    cols = []
    for w in POOL_WINDOWS:
        cnt = np.clip(t - w // 2 + w, 0, n) - np.clip(t - w // 2, 0, n)
        cols.append(np.repeat((1.0 / cnt)[:, None], POOL_GROUP, axis=1))
    return np.concatenate(cols, axis=1).astype(np.float32)


def _block_diag(w):
    nl, g, d, _ = w.shape
    eye = jnp.eye(g, dtype=w.dtype)
    return (eye[None, :, None, :, None] * w[:, :, :, None, :]).reshape(nl, g * d, g * d)


def kernel(x, c, ctx, c_ctx, w_mod, b_mod, norm_g, w_in, ret_decay, conv_w, pool_w, pool_scale,
           w_branch, w_o, ffn_w_up, ffn_w_down):
    b, n, _ = x.shape
    n_ctx = ctx.shape[1]

    cc_rows = jnp.concatenate(
        [c, c_ctx[None, :], jnp.zeros((MOD_ROWS - b - 1, D_MODEL), F32)], axis=0)
    mod = _modulation(cc_rows, w_mod, b_mod).reshape(DEPTH, MOD_ROWS, 6, D_MODEL)
    ctx_row = b

    cos_l, sin_l = (jnp.asarray(t) for t in _rope_tables(n))
    cos_c, sin_c = jnp.ones((n_ctx, MIX), F32), jnp.zeros((n_ctx, MIX), F32)
    dft_l, cc_t, ss_t = (jnp.asarray(t).astype(BF16) for t in _dft_tables(n))
    dft_c = jnp.asarray(_dft_tables(n_ctx)[0]).astype(BF16)
    icnt_l, icnt_c = jnp.asarray(_pool_inv_counts(n)), jnp.asarray(_pool_inv_counts(n_ctx))
    avg = jnp.asarray(np.kron(np.eye(RET_HEADS), np.full((RET_HEAD_DIM, RET_HEAD_DIM),
                                                         1.0 / RET_HEAD_DIM)), F32).astype(BF16)
    s_zero = jnp.zeros((b, 2, MIX, MIX), F32)

    w_in_b = w_in.astype(BF16)
    wb = w_branch.astype(BF16)
    wo = w_o.astype(BF16)
    wup = ffn_w_up.astype(BF16)
    wdn = ffn_w_down.astype(BF16)
    pool_bd = _block_diag(pool_w).astype(BF16)
    rd_lane = jnp.repeat(ret_decay.astype(F32), RET_HEAD_DIM, axis=2)
    ps = pool_scale.reshape(DEPTH, 1, MIX)

    for l in range(DEPTH):
        need_ctx = l < DEPTH - 1
        pa_c, pg_c = _in_proj(ctx, mod, norm_g, w_in_b, l, ctx_row, tm=n_ctx)
        pa_l, pg_l = _in_proj(x, mod, norm_g, w_in_b, l, None, tm=512)

        ret_c, s_ctx = _retention(pa_c, cos_c, sin_c, rd_lane, s_zero, avg, l)
        ret_l, _ = _retention(pa_l, cos_l, sin_l, rd_lane, s_ctx, avg, l)

        four_l = _fourier(pa_l, cc_t, ss_t, dft_l)
        conv_l, pool_l = _conv_pool(pa_l, conv_w, pool_bd, ps, icnt_l, l)
        x = _post((ret_l, conv_l, four_l, pool_l), pg_l, x, mod, norm_g,
                  wb, wo, wup, wdn, l, None, tm=512)

        if need_ctx:
            four_c = _fourier(pa_c, cc_t, ss_t, dft_c)
            conv_c, pool_c = _conv_pool(pa_c, conv_w, pool_bd, ps, icnt_c, l)
            ctx = _post((ret_c, conv_c, four_c, pool_c), pg_c, ctx, mod, norm_g,
                        wb, wo, wup, wdn, l, ctx_row, tm=n_ctx)
    return x
```

```python
import cmath
import functools
import math

import numpy as np
import jax
import jax.numpy as jnp
from jax import lax
from jax.experimental import pallas as pl
from jax.experimental.pallas import tpu as pltpu

D_MODEL = 1024
DEPTH = 4
GRID_W = 64
MIX = 256
N_BRANCH = 4
RET_HEADS = 4
RET_HEAD_DIM = MIX // RET_HEADS
POOL_WINDOWS = (2, 4, 8, 16)
POOL_GROUP = MIX // len(POOL_WINDOWS)
FFT_GROUPS = 4
FFT_GROUP_DIM = MIX // FFT_GROUPS
D_FF = 2816
PA_WIDTH = 9 * MIX
PG_WIDTH = N_BRANCH * D_MODEL
ROPE_BASE = 10000.0
EPS = 1e-6

LANES = 128
RET_CHUNK = 256
DFT_BLOCK = 256
SEQ_PAD = 16
EDGE_ROWS = 8
ROW_CHUNK = 512
CONV_ROWS = 128
MOD_ROWS = 16
VMEM_LIMIT = 60 * 1024 * 1024

F32 = jnp.float32
BF16 = jnp.bfloat16


def _dot(a, b):
    return jnp.dot(a, b, preferred_element_type=F32)


def _rms(xf, g):
    ms = jnp.mean(xf * xf, axis=-1, keepdims=True)
    return xf * lax.rsqrt(ms + EPS) * g


def _silu(x):
    return x * jax.nn.sigmoid(x)


def _const_spec(shape):
    zeros = (0,) * len(shape)
    return pl.BlockSpec(shape, lambda *_: zeros, pipeline_mode=pl.Buffered(1))


def _layer_spec(shape, layer):
    tail = (0,) * (len(shape) - 1)
    return pl.BlockSpec((1,) + tuple(shape[1:]), lambda *_: (layer,) + tail,
                        pipeline_mode=pl.Buffered(1))


def _mod_spec(layer, row):
    return pl.BlockSpec((1, 1, 6, D_MODEL),
                        lambda i, j: (layer, i if row is None else row, 0, 0))


def _params(*sem):
    return pltpu.CompilerParams(dimension_semantics=sem, vmem_limit_bytes=VMEM_LIMIT)


def _mod_kernel(c_ref, w_ref, b_ref, o_ref):
    a = _silu(c_ref[...]).astype(BF16)
    o_ref[0] = _dot(a, w_ref[0].astype(BF16)) + b_ref[0]


def _modulation(cc, w_mod, b_mod):
    tn = 1536
    n_out = w_mod.shape[-1]
    return pl.pallas_call(
        _mod_kernel,
        out_shape=jax.ShapeDtypeStruct((DEPTH, MOD_ROWS, n_out), F32),
        grid=(DEPTH, n_out // tn),
        in_specs=[
            pl.BlockSpec((MOD_ROWS, D_MODEL), lambda l, j: (0, 0)),
            pl.BlockSpec((1, D_MODEL, tn), lambda l, j: (l, 0, j)),
            pl.BlockSpec((1, 1, tn), lambda l, j: (l, 0, j)),
        ],
        out_specs=pl.BlockSpec((1, MOD_ROWS, tn), lambda l, j: (l, 0, j)),
        compiler_params=_params("parallel", "parallel"),
        name="modulation",
    )(cc, w_mod, b_mod.reshape(DEPTH, 1, n_out))


def _in_kernel(x_ref, mod_ref, g_ref, w_ref, cos_ref, sin_ref, h_ref, pa_ref):
    mod = mod_ref[0, 0]
    h = _rms(x_ref[0], g_ref[0, 0:1] * (1.0 + mod[1:2])) + mod[0:1]
    hb = h.astype(BF16)
    h_ref[0] = hb
    cos, sin = cos_ref[...], sin_ref[...]
    first_of_pair = (lax.broadcasted_iota(jnp.int32, cos.shape, 1) & 16) == 0

    def rope(t, scale):
        partner = jnp.where(first_of_pair, pltpu.roll(t, MIX - 16, 1), pltpu.roll(t, 16, 1))
        return (t * cos + partner * sin) * scale

    for j in range(0, PA_WIDTH, 3 * MIX):
        res = _dot(hb, w_ref[0, :, j:j + 3 * MIX])
        if j == 0:
            pa_ref[0, :, 0:MIX] = rope(res[:, 0:MIX], RET_HEAD_DIM ** -0.5).astype(BF16)
            pa_ref[0, :, MIX:2 * MIX] = rope(res[:, MIX:2 * MIX], 1.0).astype(BF16)
            pa_ref[0, :, 2 * MIX:3 * MIX] = res[:, 2 * MIX:3 * MIX].astype(BF16)
        else:
            pa_ref[0, :, j:j + 3 * MIX] = res.astype(BF16)


def _in_proj(x, mod, norm_g, wa, cos, sin, layer, mod_row, tm):
    b, n, _ = x.shape
    return pl.pallas_call(
        _in_kernel,
        out_shape=(jax.ShapeDtypeStruct((b, n, D_MODEL), BF16),
                   jax.ShapeDtypeStruct((b, n, PA_WIDTH), BF16)),
        grid=(b, n // tm),
        in_specs=[
            pl.BlockSpec((1, tm, D_MODEL), lambda i, j: (i, j, 0)),
            _mod_spec(layer, mod_row),
            _layer_spec(norm_g.shape, layer),
            _layer_spec(wa.shape, layer),
            pl.BlockSpec((tm, MIX), lambda i, j: (j, 0)),
            pl.BlockSpec((tm, MIX), lambda i, j: (j, 0)),
        ],
        out_specs=(pl.BlockSpec((1, tm, D_MODEL), lambda i, j: (i, j, 0)),
                   pl.BlockSpec((1, tm, PA_WIDTH), lambda i, j: (i, j, 0))),
        compiler_params=_params("parallel", "parallel"),
        name="in_proj",
    )(x, mod, norm_g, wa, cos, sin)


def _retention_step(q_ref, k_ref, v_ref, g_ref, rd_ref, s0_ref, avg_ref, o_ref, sout_ref,
                    sf_s, sb_s, dm_s, dec_s, *, n, fill):
    c_len = RET_CHUNK
    nc = n // c_len
    lane = lax.broadcasted_iota(jnp.int32, (c_len, MIX), 1)
    head = lane // RET_HEAD_DIM

    @pl.when(pl.program_id(0) == 0)
    def _decay_tables():
        rd = rd_ref[0]
        lg = jnp.minimum(rd, 0.0) - jnp.log1p(jnp.exp(-jnp.abs(rd)))
        lgf, lgb = lg[0:1], lg[1:2]
        idx = lax.broadcasted_iota(jnp.int32, (c_len, MIX), 0).astype(F32)
        dec_s[0] = jnp.exp(lgf * (idx + 1.0))
        dec_s[1] = jnp.exp(lgf * (c_len - 1.0 - idx))
        dec_s[2] = jnp.exp(lgb * (c_len - idx))
        dec_s[3] = jnp.exp(lgb * idx)
        dec_s[4] = jnp.exp(lgf * float(c_len)) + 0.0 * idx
        dec_s[5] = jnp.exp(lgb * float(c_len)) + 0.0 * idx
        ii = lax.broadcasted_iota(jnp.int32, (c_len, c_len), 0)
        jj = lax.broadcasted_iota(jnp.int32, (c_len, c_len), 1)
        diff = (ii - jj).astype(F32)
        for h in range(RET_HEADS):
            lo = h * RET_HEAD_DIM
            rate = jnp.where(diff >= 0.0, lgf[:, lo:lo + 1], -lgb[:, lo:lo + 1])
            dm_s[h] = jnp.exp(rate * diff)

    row_head = lax.broadcasted_iota(jnp.int32, (MIX, MIX), 0) // RET_HEAD_DIM
    col_head = lax.broadcasted_iota(jnp.int32, (MIX, MIX), 1) // RET_HEAD_DIM
    block_diag = row_head == col_head
    tn_dims = (((0,), (0,)), ((), ()))

    state = s0_ref[0, 0]
    for c in range(nc):
        rows = pl.ds(c * c_len, c_len)
        sf_s[c] = state.astype(BF16)
        kd = (k_ref[0, rows, :].astype(F32) * dec_s[1]).astype(BF16)
        upd = lax.dot_general(kd, v_ref[0, rows, :], tn_dims, preferred_element_type=F32)
        state = state * dec_s[4] + jnp.where(block_diag, upd, 0.0)
    sout_ref[0, 0] = state
    state = s0_ref[0, 1]
    for c in range(nc - 1, -1, -1):
        rows = pl.ds(c * c_len, c_len)
        sb_s[c] = state.astype(BF16)
        kd = (k_ref[0, rows, :].astype(F32) * dec_s[3]).astype(BF16)
        upd = lax.dot_general(kd, v_ref[0, rows, :], tn_dims, preferred_element_type=F32)
        state = state * dec_s[5] + jnp.where(block_diag, upd, 0.0)
    sout_ref[0, 1] = state

    nt_dims = (((1,), (1,)), ((), ()))
    avg = avg_ref[...]
    for c in range(nc):
        fill()
        rows = pl.ds(c * c_len, c_len)
        q = q_ref[0, rows, :]
        k = k_ref[0, rows, :]
        v = v_ref[0, rows, :]
        qf = q.astype(F32)
        y = (_dot((qf * dec_s[0]).astype(BF16), sf_s[c])
             + _dot((qf * dec_s[2]).astype(BF16), sb_s[c]))
        probs, vals = [], []
        for h in range(RET_HEADS):
            in_head = head == h
            s = lax.dot_general(jnp.where(in_head, q, jnp.zeros_like(q)), k, nt_dims,
                                preferred_element_type=F32)
            probs.append((s * dm_s[h]).astype(BF16))
            vals.append(jnp.where(in_head, v, jnp.zeros_like(v)))
        y = y + _dot(jnp.concatenate(probs, axis=1), jnp.concatenate(vals, axis=0))
        fill()
        d = y - _dot(y.astype(BF16), avg)
        yn = d * lax.rsqrt(_dot((d * d).astype(BF16), avg) + EPS)
        o_ref[0, rows, :] = (yn * _silu(g_ref[0, rows, :].astype(F32))).astype(BF16)


def _padded_rows(ref, r0, rows, n):
    zeros = jnp.zeros((SEQ_PAD, MIX), ref.dtype)
    top = zeros if r0 == 0 else ref[0, r0 - SEQ_PAD:r0, :]
    bot = zeros if r0 + rows == n else ref[0, r0 + rows:r0 + rows + SEQ_PAD, :]
    return jnp.concatenate([top, ref[0, r0:r0 + rows, :], bot], axis=0).astype(F32)


def _shift_rows(t, s):
    return pltpu.roll(t, s % t.shape[0], 0)


def _conv_pool_step(cb_ref, cg_ref, cx_ref, pu_ref, cw_ref, pw_ref, ps_ref, norm_ref,
                    conv_ref, pool_ref, *, n, fill):
    rows = min(n, CONV_ROWS)
    main = slice(SEQ_PAD, SEQ_PAD + rows)
    cw = cw_ref[0]
    in_first_group = lax.broadcasted_iota(jnp.int32, (rows, LANES), 1) < POOL_GROUP
    for r0 in range(0, n, rows):
        fill()
        u = _padded_rows(cg_ref, r0, rows, n) * _padded_rows(cx_ref, r0, rows, n)
        conv = _shift_rows(u, 1) * cw[0:1] + u * cw[1:2] + _shift_rows(u, -1) * cw[2:3]
        conv_ref[0, r0:r0 + rows, :] = (
            cb_ref[0, r0:r0 + rows, :].astype(F32) * conv[main]).astype(BF16)

        e_all = _padded_rows(pu_ref, r0, rows, n)
        halves = []
        for half in range(MIX // LANES):
            e = e_all[:, half * LANES:(half + 1) * LANES]
            c = e + _shift_rows(e, 1)
            sums = [c]
            for w in POOL_WINDOWS[1:2 * half + 2]:
                c = _shift_rows(c, w // 4) + _shift_rows(c, -(w // 4))
                sums.append(c)
            halves.append(jnp.where(in_first_group, sums[-2][main], sums[-1][main]))
        win = jnp.concatenate(halves, axis=1)
        mean = win * norm_ref[1, 0:1]
        if r0 == 0:
            mean = jnp.concatenate([win[:EDGE_ROWS] * norm_ref[0], mean[EDGE_ROWS:]], axis=0)
        if r0 + rows == n:
            mean = jnp.concatenate([mean[:-EDGE_ROWS], win[-EDGE_ROWS:] * norm_ref[2]], axis=0)
        pooled = (mean - e_all[main]).astype(BF16)
        pool_ref[0, r0:r0 + rows, :] = (_dot(pooled, pw_ref[0]) * ps_ref[0]).astype(BF16)


def _times_const(z, w):
    re, im = z

    def lin(a, ca, b, cb):
        acc = None
        for t, coef in ((a, ca), (b, cb)):
            if abs(coef) < 1e-12:
                continue
            unit = abs(abs(coef) - 1.0) < 1e-12
            term = t if unit else t * abs(coef)
            if acc is None:
                acc = term if coef > 0 else -term
            else:
                acc = acc + term if coef > 0 else acc - term
        return acc

    return lin(re, w.real, im, -w.imag), lin(re, w.imag, im, w.real)


def _dft_blocks(xs):
    count = len(xs)
    if count == 1:
        return xs
    even, odd = _dft_blocks(xs[0::2]), _dft_blocks(xs[1::2])
    out = [None] * count
    for k in range(count // 2):
        t = _times_const(odd[k], cmath.exp(-2j * math.pi * k / count))
        out[k] = (even[k][0] + t[0], even[k][1] + t[1])
        out[k + count // 2] = (even[k][0] - t[0], even[k][1] - t[1])
    return out


def _fourier_step(u_ref, cc_ref, ssn_ref, tw_ref, dft_ref, o_ref, *, n, fill):
    m = DFT_BLOCK
    r_cnt = n // m
    scale = 1.0 / math.sqrt(n * FFT_GROUP_DIM)
    blocks = []
    for r in range(r_cnt):
        u = u_ref[0, r * m:(r + 1) * m, :]
        blocks.append((_dot(u, cc_ref[...]), _dot(u, ssn_ref[...])))
        fill()
    spectra = _dft_blocks(blocks)
    for k1 in range(r_cnt):
        fill()
        re, im = spectra[k1]
        if k1 > 0:
            tc = jnp.concatenate([tw_ref[0, k1]] * (MIX // LANES), axis=1)
            ts = jnp.concatenate([tw_ref[1, k1]] * (MIX // LANES), axis=1)
            re, im = re * tc + im * ts, im * tc - re * ts
        rhs = jnp.concatenate([re.astype(BF16), im.astype(BF16)], axis=0)
        o_ref[0, :, k1 * MIX:(k1 + 1) * MIX] = (_dot(dft_ref[...], rhs) * scale).astype(BF16)


def _mixer_kernel(h_ref, wg_ref, a_ref, b_ref, c_ref, d_ref, rd_ref, s0_ref, avg_ref,
                  cw_ref, pw_ref, ps_ref, norm_ref, cc_ref, ssn_ref, tw_ref, dft_ref,
                  pg_ref, ret_ref, sout_ref, conv_ref, pool_ref, four_ref,
                  sf_s, sb_s, dm_s, dec_s, *, n):
    step = pl.program_id(1)
    rows = min(n, ROW_CHUNK)

    def gate_filler(j):
        pieces = iter([(r0, c0) for r0 in range(0, n, rows) for c0 in range(0, D_MODEL, MIX)])

        def fill():
            r0, c0 = next(pieces, (None, None))
            if r0 is not None:
                w = wg_ref[0, :, j * D_MODEL + c0:j * D_MODEL + c0 + MIX]
                gate = jax.nn.sigmoid(_dot(h_ref[0, r0:r0 + rows, :], w))
                pg_ref[0, r0:r0 + rows, c0:c0 + MIX] = gate.astype(BF16)

        def drain():
            for _ in pieces_total:
                fill()

        pieces_total = range((n // rows) * (D_MODEL // MIX))
        return fill, drain

    @pl.when(step == 0)
    def _():
        fill, drain = gate_filler(0)
        _retention_step(a_ref, b_ref, c_ref, d_ref, rd_ref, s0_ref, avg_ref, ret_ref, sout_ref,
                        sf_s, sb_s, dm_s, dec_s, n=n, fill=fill)
        drain()

    @pl.when(step == 1)
    def _():
        fill, drain = gate_filler(1)
        _conv_pool_step(a_ref, b_ref, c_ref, d_ref, cw_ref, pw_ref, ps_ref, norm_ref,
                        conv_ref, pool_ref, n=n, fill=fill)
        drain()

    @pl.when(step == 2)
    def _():
        fill, drain = gate_filler(2)
        _fourier_step(a_ref, cc_ref, ssn_ref, tw_ref, dft_ref, four_ref, n=n, fill=fill)
        drain()

    @pl.when(step == 3)
    def _():
        gate_filler(3)[1]()


_SLOT_COLUMNS = ((0, 4, 7, 7), (1, 5, 7, 7), (2, 6, 7, 7), (3, 8, 7, 7))


def _mixer(h, pa, wg, rd_lane, s0, avg, conv_w, pool_bd, pool_scale, pool_norm,
           cc, ssn, tw, dft, layer):
    b, n, _ = pa.shape
    nc = n // RET_CHUNK
    r_cnt = n // DFT_BLOCK

    def slot(cols):
        def index(i, j):
            col = jnp.where(j == 0, cols[0], jnp.where(j == 1, cols[1], cols[2]))
            return (i, 0, col)
        return pl.BlockSpec((1, n, MIX), index)

    seq_out = pl.BlockSpec((1, n, MIX), lambda i, j: (i, 0, 0))
    state = pl.BlockSpec((1, 2, MIX, MIX), lambda i, j: (i, 0, 0, 0))
    return pl.pallas_call(
        functools.partial(_mixer_kernel, n=n),
        out_shape=(jax.ShapeDtypeStruct((b, n, PG_WIDTH), BF16),
                   jax.ShapeDtypeStruct((b, n, MIX), BF16),
                   jax.ShapeDtypeStruct((b, 2, MIX, MIX), F32),
                   jax.ShapeDtypeStruct((b, n, MIX), BF16),
                   jax.ShapeDtypeStruct((b, n, MIX), BF16),
                   jax.ShapeDtypeStruct((b, DFT_BLOCK, r_cnt * MIX), BF16)),
        grid=(b, 4),
        in_specs=[pl.BlockSpec((1, n, D_MODEL), lambda i, j: (i, 0, 0)),
                  _layer_spec(wg.shape, layer),
                  slot(_SLOT_COLUMNS[0]), slot(_SLOT_COLUMNS[1]),
                  slot(_SLOT_COLUMNS[2]), slot(_SLOT_COLUMNS[3]),
                  _layer_spec(rd_lane.shape, layer), state, _const_spec(avg.shape),
                  _layer_spec(conv_w.shape, layer), _layer_spec(pool_bd.shape, layer),
                  _layer_spec(pool_scale.shape, layer), _const_spec(pool_norm.shape),
                  _const_spec(cc.shape), _const_spec(ssn.shape), _const_spec(tw.shape),
                  _const_spec(dft.shape)],
        out_specs=(pl.BlockSpec((1, n, D_MODEL), lambda i, j: (i, 0, j)),
                   seq_out, state, seq_out, seq_out,
                   pl.BlockSpec((1, DFT_BLOCK, r_cnt * MIX), lambda i, j: (i, 0, 0))),
        scratch_shapes=[pltpu.VMEM((nc, MIX, MIX), BF16), pltpu.VMEM((nc, MIX, MIX), BF16),
                        pltpu.VMEM((RET_HEADS, RET_CHUNK, RET_CHUNK), F32),
                        pltpu.VMEM((6, RET_CHUNK, MIX), F32)],
        compiler_params=_params("arbitrary", "arbitrary"),
        name="mixer",
    )(h, wg, pa, pa, pa, pa, rd_lane, s0, avg, conv_w, pool_bd, pool_scale, pool_norm,
      cc, ssn, tw, dft)


FFN_SPLITS = (0, 1536, D_FF)


def _post_kernel(b0_ref, b1_ref, b2_ref, b3_ref, pg_ref, x_ref, mod_ref, g_ref,
                 wb_ref, wo_ref, wup_ref, wdn_ref, o_ref):
    mod = mod_ref[0, 0]
    g = g_ref[0]
    merged = None
    for i, br in enumerate((b0_ref, b1_ref, b2_ref, b3_ref)):
        gate = pg_ref[0, :, i * D_MODEL:(i + 1) * D_MODEL].astype(F32)
        t = gate * _dot(br[0], wb_ref[0, i])
        merged = t if merged is None else merged + t
    mix = _dot(merged.astype(BF16), wo_ref[0])
    x1 = x_ref[0] + _rms(mix, g[1:2] * mod[2:3])

    h = (_rms(x1, g[2:3] * (1.0 + mod[4:5])) + mod[3:4]).astype(BF16)
    ffn = None
    for lo, hi in zip(FFN_SPLITS[:-1], FFN_SPLITS[1:]):
        a = _dot(h, wup_ref[0, :, lo:hi])
        u = _dot(h, wup_ref[0, :, D_FF + lo:D_FF + hi])
        t = _dot((_silu(a) * u).astype(BF16), wdn_ref[0, lo:hi, :])
        ffn = t if ffn is None else ffn + t
    o_ref[0] = x1 + _rms(ffn, g[3:4] * mod[5:6])


def _post(branches, pg, x, mod, norm_g, wb, wo, wup, wdn, layer, mod_row, tm):
    b, n, _ = x.shape
    br = pl.BlockSpec((1, tm, MIX), lambda i, j: (i, j, 0))
    return pl.pallas_call(
        _post_kernel,
        out_shape=jax.ShapeDtypeStruct((b, n, D_MODEL), F32),
        grid=(b, n // tm),
        in_specs=[br, br, br, br,
                  pl.BlockSpec((1, tm, PG_WIDTH), lambda i, j: (i, j, 0)),
                  pl.BlockSpec((1, tm, D_MODEL), lambda i, j: (i, j, 0)),
                  _mod_spec(layer, mod_row),
                  _layer_spec(norm_g.shape, layer),
                  _layer_spec(wb.shape, layer),
                  _layer_spec(wo.shape, layer),
                  _layer_spec(wup.shape, layer),
                  _layer_spec(wdn.shape, layer)],
        out_specs=pl.BlockSpec((1, tm, D_MODEL), lambda i, j: (i, j, 0)),
        compiler_params=_params("parallel", "parallel"),
        name="merge_ffn",
    )(*branches, pg, x, mod, norm_g, wb, wo, wup, wdn)


def _rope_tables(n):
    rows = np.repeat(np.arange(n // GRID_W, dtype=np.float64), GRID_W)
    cols = np.tile(np.arange(GRID_W, dtype=np.float64), n // GRID_W)
    lane = np.arange(MIX)
    in_head = lane % RET_HEAD_DIM
    pos = np.where((in_head < RET_HEAD_DIM // 2)[None, :], rows[:, None], cols[:, None])
    quarter = RET_HEAD_DIM // 4
    freq = ROPE_BASE ** (-np.arange(quarter, dtype=np.float64) / quarter)
    ang = pos * freq[lane % quarter][None, :]
    sign = np.where((lane % (2 * quarter)) < quarter, -1.0, 1.0)[None, :]
    return np.cos(ang).astype(np.float32), (np.sin(ang) * sign).astype(np.float32)


def _dft_tables(n):
    m = DFT_BLOCK
    r_cnt = n // m
    k = np.arange(m, dtype=np.int64)
    ang = 2.0 * np.pi * ((k[:, None] * k[None, :]) % m) / m
    dft = np.concatenate([np.cos(ang), np.sin(ang)], axis=1).astype(np.float32)
    angt = 2.0 * np.pi * (np.arange(r_cnt)[:, None] * k[None, :]) / n
    tw = np.stack([np.cos(angt), np.sin(angt)])[..., None].repeat(LANES, axis=-1).astype(np.float32)
    ch = np.arange(FFT_GROUP_DIM, dtype=np.int64)
    angc = 2.0 * np.pi * ((ch[:, None] * ch[None, :]) % FFT_GROUP_DIM) / FFT_GROUP_DIM
    eye = np.eye(FFT_GROUPS)
    cc = np.kron(eye, np.cos(angc)).astype(np.float32)
    ssn = np.kron(eye, -np.sin(angc)).astype(np.float32)
    return dft, tw, cc, ssn


def _pool_norm(n):
    t = np.arange(n)
    cols = []
    for w in POOL_WINDOWS:
        cnt = np.clip(t - w // 2 + w, 0, n) - np.clip(t - w // 2, 0, n)
        inv = 1.0 / cnt
        assert np.all(cnt[EDGE_ROWS:n - EDGE_ROWS] == w)
        col = np.stack([inv[:EDGE_ROWS], np.full(EDGE_ROWS, 1.0 / w), inv[n - EDGE_ROWS:]])
        cols.append(np.repeat(col[:, :, None], POOL_GROUP, axis=2))
    return np.concatenate(cols, axis=2).astype(np.float32)


def _block_diag(w):
    nl, g, d, _ = w.shape
    eye = jnp.eye(g, dtype=w.dtype)
    return (eye[None, :, None, :, None] * w[:, :, :, None, :]).reshape(nl, g * d, g * d)


def kernel(x, c, ctx, c_ctx, w_mod, b_mod, norm_g, w_in, ret_decay, conv_w, pool_w, pool_scale,
           w_branch, w_o, ffn_w_up, ffn_w_down):
    b, n, _ = x.shape
    n_ctx = ctx.shape[1]

    cc_rows = jnp.concatenate(
        [c, c_ctx[None, :], jnp.zeros((MOD_ROWS - b - 1, D_MODEL), F32)], axis=0)
    mod = _modulation(cc_rows, w_mod, b_mod).reshape(DEPTH, MOD_ROWS, 6, D_MODEL)
    ctx_row = b

    cos_l, sin_l = (jnp.asarray(t) for t in _rope_tables(n))
    cos_c, sin_c = jnp.ones((n_ctx, MIX), F32), jnp.zeros((n_ctx, MIX), F32)

    def fourier_tables(length):
        dft, tw, cc, ssn = _dft_tables(length)
        return (jnp.asarray(cc).astype(BF16), jnp.asarray(ssn).astype(BF16), jnp.asarray(tw),
                jnp.asarray(dft).astype(BF16))

    four_l, four_c = fourier_tables(n), fourier_tables(n_ctx)
    norm_l, norm_c = jnp.asarray(_pool_norm(n)), jnp.asarray(_pool_norm(n_ctx))
    avg = jnp.asarray(np.kron(np.eye(RET_HEADS), np.full((RET_HEAD_DIM, RET_HEAD_DIM),
                                                         1.0 / RET_HEAD_DIM)), F32).astype(BF16)
    s_zero = jnp.zeros((b, 2, MIX, MIX), F32)

    wa = w_in[:, :, :PA_WIDTH].astype(BF16)
    wg = w_in[:, :, PA_WIDTH:].astype(BF16)
    wb = w_branch.astype(BF16)
    wo = w_o.astype(BF16)
    wup = ffn_w_up.astype(BF16)
    wdn = ffn_w_down.astype(BF16)
    pool_bd = _block_diag(pool_w).astype(BF16)
    rd_lane = jnp.repeat(ret_decay.astype(F32), RET_HEAD_DIM, axis=2)
    ps = pool_scale.reshape(DEPTH, 1, MIX)

    for l in range(DEPTH):
        need_ctx = l < DEPTH - 1
        h_c, pa_c = _in_proj(ctx, mod, norm_g, wa, cos_c, sin_c, l, ctx_row, tm=n_ctx)
        h_l, pa_l = _in_proj(x, mod, norm_g, wa, cos_l, sin_l, l, None, tm=1024)

        pg_c, ret_c, s_ctx, conv_c, pool_c, fo_c = _mixer(
            h_c, pa_c, wg, rd_lane, s_zero, avg, conv_w, pool_bd, ps, norm_c, *four_c, l)
        pg_l, ret_l, _, conv_l, pool_l, fo_l = _mixer(
            h_l, pa_l, wg, rd_lane, s_ctx, avg, conv_w, pool_bd, ps, norm_l, *four_l, l)

        x = _post((ret_l, conv_l, fo_l.reshape(b, n, MIX), pool_l), pg_l, x, mod, norm_g,
                  wb, wo, wup, wdn, l, None, tm=512)
        if need_ctx:
            ctx = _post((ret_c, conv_c, fo_c.reshape(b, n_ctx, MIX), pool_c), pg_c, ctx, mod,
                        norm_g, wb, wo, wup, wdn, l, ctx_row, tm=n_ctx)
    return x
```

```python
import cmath
import functools
import math

import numpy as np
import jax
import jax.numpy as jnp
from jax import lax
from jax.experimental import pallas as pl
from jax.experimental.pallas import tpu as pltpu

D_MODEL = 1024
DEPTH = 4
GRID_W = 64
MIX = 256
N_BRANCH = 4
RET_HEADS = 4
RET_HEAD_DIM = MIX // RET_HEADS
POOL_WINDOWS = (2, 4, 8, 16)
POOL_GROUP = MIX // len(POOL_WINDOWS)
FFT_GROUPS = 4
FFT_GROUP_DIM = MIX // FFT_GROUPS
D_FF = 2816
PA_WIDTH = 9 * MIX
PG_WIDTH = N_BRANCH * D_MODEL
ROPE_BASE = 10000.0
EPS = 1e-6

LANES = 128
RET_CHUNK = 256
RET_GROUP = 2
DFT_BLOCK = 256
SEQ_PAD = 16
EDGE_ROWS = 8
ROW_CHUNK = 512
CONV_ROWS = 128
MOD_ROWS = 16
VMEM_LIMIT = 60 * 1024 * 1024

F32 = jnp.float32
BF16 = jnp.bfloat16


def _dot(a, b):
    return jnp.dot(a, b, preferred_element_type=F32)


def _rms(xf, g):
    ms = jnp.mean(xf * xf, axis=-1, keepdims=True)
    return xf * lax.rsqrt(ms + EPS) * g


def _silu(x):
    return x * jax.nn.sigmoid(x)


def _const_spec(shape):
    zeros = (0,) * len(shape)
    return pl.BlockSpec(shape, lambda *_: zeros, pipeline_mode=pl.Buffered(1))


def _layer_spec(shape, layer):
    tail = (0,) * (len(shape) - 1)
    return pl.BlockSpec((1,) + tuple(shape[1:]), lambda *_: (layer,) + tail,
                        pipeline_mode=pl.Buffered(1))


def _mod_spec(layer, row):
    return pl.BlockSpec((1, 1, 6, D_MODEL),
                        lambda i, j: (layer, i if row is None else row, 0, 0))


def _params(*sem):
    return pltpu.CompilerParams(dimension_semantics=sem, vmem_limit_bytes=VMEM_LIMIT)


def _mod_kernel(c_ref, w_ref, b_ref, o_ref):
    a = _silu(c_ref[...]).astype(BF16)
    o_ref[0] = _dot(a, w_ref[0].astype(BF16)) + b_ref[0]


def _modulation(cc, w_mod, b_mod):
    tn = 1536
    n_out = w_mod.shape[-1]
    return pl.pallas_call(
        _mod_kernel,
        out_shape=jax.ShapeDtypeStruct((DEPTH, MOD_ROWS, n_out), F32),
        grid=(DEPTH, n_out // tn),
        in_specs=[
            pl.BlockSpec((MOD_ROWS, D_MODEL), lambda l, j: (0, 0)),
            pl.BlockSpec((1, D_MODEL, tn), lambda l, j: (l, 0, j)),
            pl.BlockSpec((1, 1, tn), lambda l, j: (l, 0, j)),
        ],
        out_specs=pl.BlockSpec((1, MOD_ROWS, tn), lambda l, j: (l, 0, j)),
        compiler_params=_params("parallel", "parallel"),
        name="modulation",
    )(cc, w_mod, b_mod.reshape(DEPTH, 1, n_out))


def _in_kernel(x_ref, mod_ref, g_ref, w_ref, cos_ref, sin_ref, h_ref, pa_ref):
    mod = mod_ref[0, 0]
    h = _rms(x_ref[0], g_ref[0, 0:1] * (1.0 + mod[1:2])) + mod[0:1]
    hb = h.astype(BF16)
    h_ref[0] = hb
    cos, sin = cos_ref[...], sin_ref[...]
    first_of_pair = (lax.broadcasted_iota(jnp.int32, cos.shape, 1) & 16) == 0

    def rope(t, scale):
        partner = jnp.where(first_of_pair, pltpu.roll(t, MIX - 16, 1), pltpu.roll(t, 16, 1))
        return (t * cos + partner * sin) * scale

    for j in range(0, pa_ref.shape[-1], 3 * MIX):
        res = _dot(hb, w_ref[0, :, j:j + 3 * MIX])
        if j == 0:
            pa_ref[0, :, 0:MIX] = rope(res[:, 0:MIX], RET_HEAD_DIM ** -0.5).astype(BF16)
            pa_ref[0, :, MIX:2 * MIX] = rope(res[:, MIX:2 * MIX], 1.0).astype(BF16)
            pa_ref[0, :, 2 * MIX:3 * MIX] = res[:, 2 * MIX:3 * MIX].astype(BF16)
        else:
            pa_ref[0, :, j:j + 3 * MIX] = res.astype(BF16)


def _in_proj(x, mod, norm_g, w_in, cos, sin, layer, mod_row, tm, pa_width=PA_WIDTH):
    b, n, _ = x.shape
    return pl.pallas_call(
        _in_kernel,
        out_shape=(jax.ShapeDtypeStruct((b, n, D_MODEL), BF16),
                   jax.ShapeDtypeStruct((b, n, pa_width), BF16)),
        grid=(b, n // tm),
        in_specs=[
            pl.BlockSpec((1, tm, D_MODEL), lambda i, j: (i, j, 0)),
            _mod_spec(layer, mod_row),
            _layer_spec(norm_g.shape, layer),
            pl.BlockSpec((1, D_MODEL, pa_width), lambda i, j: (layer, 0, 0),
                         pipeline_mode=pl.Buffered(1)),
            pl.BlockSpec((tm, MIX), lambda i, j: (j, 0)),
            pl.BlockSpec((tm, MIX), lambda i, j: (j, 0)),
        ],
        out_specs=(pl.BlockSpec((1, tm, D_MODEL), lambda i, j: (i, j, 0)),
                   pl.BlockSpec((1, tm, pa_width), lambda i, j: (i, j, 0))),
        compiler_params=_params("parallel", "parallel"),
        name="in_proj",
    )(x, mod, norm_g, w_in, cos, sin)


def _retention_states(k_ref, v_ref, rd_ref, s0_ref, sout_ref, sf_s, sb_s, dm_s, dec_s, *, n,
                      fill=lambda: None):
    c_len = RET_CHUNK
    nc = n // c_len

    @pl.when(pl.program_id(0) == 0)
    def _decay_tables():
        rd = rd_ref[0]
        lg = jnp.minimum(rd, 0.0) - jnp.log1p(jnp.exp(-jnp.abs(rd)))
        lgf, lgb = lg[0:1], lg[1:2]
        idx = lax.broadcasted_iota(jnp.int32, (c_len, MIX), 0).astype(F32)
        dec_s[0] = jnp.exp(lgf * (idx + 1.0))
        dec_s[1] = jnp.exp(lgf * (c_len - 1.0 - idx))
        dec_s[2] = jnp.exp(lgb * (c_len - idx))
        dec_s[3] = jnp.exp(lgb * idx)
        dec_s[4] = jnp.exp(lgf * float(c_len)) + 0.0 * idx
        dec_s[5] = jnp.exp(lgb * float(c_len)) + 0.0 * idx
        ii = lax.broadcasted_iota(jnp.int32, (c_len, c_len), 0)
        jj = lax.broadcasted_iota(jnp.int32, (c_len, c_len), 1)
        diff = (ii - jj).astype(F32)
        for h in range(RET_HEADS):
            lo = h * RET_HEAD_DIM
            rate = jnp.where(diff >= 0.0, lgf[:, lo:lo + 1], -lgb[:, lo:lo + 1])
            dm_s[h] = jnp.exp(rate * diff)

    row_head = lax.broadcasted_iota(jnp.int32, (MIX, MIX), 0) // RET_HEAD_DIM
    col_head = lax.broadcasted_iota(jnp.int32, (MIX, MIX), 1) // RET_HEAD_DIM
    block_diag = row_head == col_head
    tn_dims = (((0,), (0,)), ((), ()))

    def advance(state, c, key_decay, chunk_decay):
        rows = pl.ds(c * c_len, c_len)
        kd = (k_ref[0, rows, :].astype(F32) * key_decay).astype(BF16)
        upd = lax.dot_general(kd, v_ref[0, rows, :], tn_dims, preferred_element_type=F32)
        return state * chunk_decay + jnp.where(block_diag, upd, 0.0)

    fwd, bwd = s0_ref[0, 0], s0_ref[0, 1]
    for i in range(nc):
        if i % 2 == 0:
            fill()
        sf_s[i] = fwd.astype(BF16)
        sb_s[nc - 1 - i] = bwd.astype(BF16)
        fwd = advance(fwd, i, dec_s[1], dec_s[4])
        bwd = advance(bwd, nc - 1 - i, dec_s[3], dec_s[5])
    sout_ref[0, 0] = fwd
    sout_ref[0, 1] = bwd


def _retention_step(q_ref, k_ref, v_ref, g_ref, rd_ref, s0_ref, avg_ref, o_ref, sout_ref,
                    sf_s, sb_s, dm_s, dec_s, *, n, fill):
    c_len = RET_CHUNK
    nc = n // c_len
    head = lax.broadcasted_iota(jnp.int32, (c_len, MIX), 1) // RET_HEAD_DIM
    _retention_states(k_ref, v_ref, rd_ref, s0_ref, sout_ref, sf_s, sb_s, dm_s, dec_s, n=n,
                      fill=fill)

    nt_dims = (((1,), (1,)), ((), ()))
    avg = avg_ref[...]
    for c0 in range(0, nc, RET_GROUP):
        group = range(c0, min(c0 + RET_GROUP, nc))
        scores, cross = {}, {}
        for c in group:
            fill()
            rows = pl.ds(c * c_len, c_len)
            q, k = q_ref[0, rows, :], k_ref[0, rows, :]
            qf = q.astype(F32)
            cross[c] = (_dot((qf * dec_s[0]).astype(BF16), sf_s[c])
                        + _dot((qf * dec_s[2]).astype(BF16), sb_s[c]))
            scores[c] = [lax.dot_general(jnp.where(head == h, q, jnp.zeros_like(q)), k, nt_dims,
                                         preferred_element_type=F32) for h in range(RET_HEADS)]
        ys = {}
        for c in group:
            v = v_ref[0, pl.ds(c * c_len, c_len), :]
            probs = [(scores[c][h] * dm_s[h]).astype(BF16) for h in range(RET_HEADS)]
            vals = [jnp.where(head == h, v, jnp.zeros_like(v)) for h in range(RET_HEADS)]
            ys[c] = cross[c] + _dot(jnp.concatenate(probs, axis=1), jnp.concatenate(vals, axis=0))
        for c in group:
            if c % 2:
                fill()
            rows = pl.ds(c * c_len, c_len)
            d = ys[c] - _dot(ys[c].astype(BF16), avg)
            yn = d * lax.rsqrt(_dot((d * d).astype(BF16), avg) + EPS)
            o_ref[0, rows, :] = (yn * _silu(g_ref[0, rows, :].astype(F32))).astype(BF16)


def _padded_rows(ref, r0, rows, n):
    zeros = jnp.zeros((SEQ_PAD, MIX), ref.dtype)
    top = zeros if r0 == 0 else ref[0, r0 - SEQ_PAD:r0, :]
    bot = zeros if r0 + rows == n else ref[0, r0 + rows:r0 + rows + SEQ_PAD, :]
    return jnp.concatenate([top, ref[0, r0:r0 + rows, :], bot], axis=0).astype(F32)


def _shift_rows(t, s):
    return pltpu.roll(t, s % t.shape[0], 0)


def _conv_step(cb_ref, cg_ref, cx_ref, cw_ref, conv_ref, *, n, fill):
    rows = min(n, CONV_ROWS)
    main = slice(SEQ_PAD, SEQ_PAD + rows)
    cw = cw_ref[0]
    for r0 in range(0, n, rows):
        fill()
        u = _padded_rows(cg_ref, r0, rows, n) * _padded_rows(cx_ref, r0, rows, n)
        conv = _shift_rows(u, 1) * cw[0:1] + u * cw[1:2] + _shift_rows(u, -1) * cw[2:3]
        conv_ref[0, r0:r0 + rows, :] = (
            cb_ref[0, r0:r0 + rows, :].astype(F32) * conv[main]).astype(BF16)


def _pool_step(pu_ref, pw_ref, ps_ref, norm_ref, pool_ref, *, n, fill):
    rows = min(n, CONV_ROWS)
    main = slice(SEQ_PAD, SEQ_PAD + rows)
    in_first_group = lax.broadcasted_iota(jnp.int32, (rows, LANES), 1) < POOL_GROUP
    for r0 in range(0, n, rows):
        fill()
        e_all = _padded_rows(pu_ref, r0, rows, n)
        halves = []
        for half in range(MIX // LANES):
            e = e_all[:, half * LANES:(half + 1) * LANES]
            c = e + _shift_rows(e, 1)
            sums = [c]
            for w in POOL_WINDOWS[1:2 * half + 2]:
                c = _shift_rows(c, w // 4) + _shift_rows(c, -(w // 4))
                sums.append(c)
            halves.append(jnp.where(in_first_group, sums[-2][main], sums[-1][main]))
        win = jnp.concatenate(halves, axis=1)
        mean = win * norm_ref[1, 0:1]
        if r0 == 0:
            mean = jnp.concatenate([win[:EDGE_ROWS] * norm_ref[0], mean[EDGE_ROWS:]], axis=0)
        if r0 + rows == n:
            mean = jnp.concatenate([mean[:-EDGE_ROWS], win[-EDGE_ROWS:] * norm_ref[2]], axis=0)
        pooled = (mean - e_all[main]).astype(BF16)
        pool_ref[0, r0:r0 + rows, :] = (_dot(pooled, pw_ref[0]) * ps_ref[0]).astype(BF16)


def _times_const(z, w):
    re, im = z

    def lin(a, ca, b, cb):
        acc = None
        for t, coef in ((a, ca), (b, cb)):
            if abs(coef) < 1e-12:
                continue
            unit = abs(abs(coef) - 1.0) < 1e-12
            term = t if unit else t * abs(coef)
            if acc is None:
                acc = term if coef > 0 else -term
            else:
                acc = acc + term if coef > 0 else acc - term
        return acc

    return lin(re, w.real, im, -w.imag), lin(re, w.imag, im, w.real)


def _dft_blocks(xs):
    count = len(xs)
    if count == 1:
        return xs
    even, odd = _dft_blocks(xs[0::2]), _dft_blocks(xs[1::2])
    out = [None] * count
    for k in range(count // 2):
        t = _times_const(odd[k], cmath.exp(-2j * math.pi * k / count))
        out[k] = (even[k][0] + t[0], even[k][1] + t[1])
        out[k + count // 2] = (even[k][0] - t[0], even[k][1] - t[1])
    return out


def _fourier_step(u_ref, cc_ref, ssn_ref, tw_ref, dft_ref, o_ref, spec_s, *, n, fill):
    m = DFT_BLOCK
    r_cnt = n // m
    scale = 1.0 / math.sqrt(n * FFT_GROUP_DIM)
    blocks = []
    for r in range(r_cnt):
        u = u_ref[0, r * m:(r + 1) * m, :]
        blocks.append((_dot(u, cc_ref[...]), _dot(u, ssn_ref[...])))
        fill()
    spectra = _dft_blocks(blocks)
    for k1 in range(r_cnt):
        fill()
        re, im = spectra[k1]
        if k1 > 0:
            tc = jnp.concatenate([tw_ref[0, k1]] * (MIX // LANES), axis=1)
            ts = jnp.concatenate([tw_ref[1, k1]] * (MIX // LANES), axis=1)
            re, im = re * tc + im * ts, im * tc - re * ts
        rhs = jnp.concatenate([re.astype(BF16), im.astype(BF16)], axis=0)
        block = _dot(dft_ref[...], rhs) * scale
        for half in range(MIX // LANES):
            spec_s[half, pl.ds(k1, m, stride=r_cnt), :] = block[:, half * LANES:(half + 1) * LANES]
    o_ref[0] = jnp.concatenate([spec_s[half] for half in range(MIX // LANES)],
                               axis=1).astype(BF16)


def _mixer_kernel(h_ref, wg_ref, a_ref, b_ref, c_ref, d_ref, rd_ref, s0_ref, avg_ref,
                  cw_ref, pw_ref, ps_ref, norm_ref, cc_ref, ssn_ref, tw_ref, dft_ref,
                  pg_ref, ret_ref, sout_ref, conv_ref, pool_ref, four_ref,
                  sf_s, sb_s, dm_s, dec_s, spec_s, *, n):
    step = pl.program_id(1)
    rows = min(n, ROW_CHUNK)

    def gate_filler(j):
        pieces = iter([(r0, c0) for r0 in range(0, n, rows) for c0 in range(0, D_MODEL, MIX)])

        def fill():
            r0, c0 = next(pieces, (None, None))
            if r0 is not None:
                w = wg_ref[0, :, j * D_MODEL + c0:j * D_MODEL + c0 + MIX]
                gate = jax.nn.sigmoid(_dot(h_ref[0, r0:r0 + rows, :], w))
                pg_ref[0, r0:r0 + rows, c0:c0 + MIX] = gate.astype(BF16)

        def drain():
            for _ in pieces_total:
                fill()

        pieces_total = range((n // rows) * (D_MODEL // MIX))
        return fill, drain

    @pl.when(step == 0)
    def _():
        fill, drain = gate_filler(0)
        _retention_step(a_ref, b_ref, c_ref, d_ref, rd_ref, s0_ref, avg_ref, ret_ref, sout_ref,
                        sf_s, sb_s, dm_s, dec_s, n=n, fill=fill)
        drain()

    @pl.when(step == 1)
    def _():
        fill, drain = gate_filler(1)
        _conv_step(a_ref, b_ref, c_ref, cw_ref, conv_ref, n=n, fill=fill)
        drain()

    @pl.when(step == 2)
    def _():
        fill, drain = gate_filler(2)
        _fourier_step(a_ref, cc_ref, ssn_ref, tw_ref, dft_ref, four_ref, spec_s, n=n, fill=fill)
        drain()

    @pl.when(step == 3)
    def _():
        fill, drain = gate_filler(3)
        _pool_step(d_ref, pw_ref, ps_ref, norm_ref, pool_ref, n=n, fill=fill)
        drain()


_SLOT_COLUMNS = ((0, 4, 7, 7), (1, 5, 5, 5), (2, 6, 6, 6), (3, 3, 3, 8))


def _mixer(h, pa, wg, rd_lane, s0, avg, conv_w, pool_bd, pool_scale, pool_norm,
           cc, ssn, tw, dft, layer):
    b, n, _ = pa.shape
    nc = n // RET_CHUNK

    def slot(cols):
        def index(i, j):
            col = jnp.where(j == 0, cols[0],
                            jnp.where(j == 1, cols[1], jnp.where(j == 2, cols[2], cols[3])))
            return (i, 0, col)
        return pl.BlockSpec((1, n, MIX), index)

    seq_out = pl.BlockSpec((1, n, MIX), lambda i, j: (i, 0, 0))
    state = pl.BlockSpec((1, 2, MIX, MIX), lambda i, j: (i, 0, 0, 0))
    return pl.pallas_call(
        functools.partial(_mixer_kernel, n=n),
        out_shape=(jax.ShapeDtypeStruct((b, n, PG_WIDTH), BF16),
                   jax.ShapeDtypeStruct((b, n, MIX), BF16),
                   jax.ShapeDtypeStruct((b, 2, MIX, MIX), F32),
                   jax.ShapeDtypeStruct((b, n, MIX), BF16),
                   jax.ShapeDtypeStruct((b, n, MIX), BF16),
                   jax.ShapeDtypeStruct((b, n, MIX), BF16)),
        grid=(b, 4),
        in_specs=[pl.BlockSpec((1, n, D_MODEL), lambda i, j: (i, 0, 0)),
                  _layer_spec(wg.shape, layer),
                  slot(_SLOT_COLUMNS[0]), slot(_SLOT_COLUMNS[1]),
                  slot(_SLOT_COLUMNS[2]), slot(_SLOT_COLUMNS[3]),
                  _layer_spec(rd_lane.shape, layer), state, _const_spec(avg.shape),
                  _layer_spec(conv_w.shape, layer), _layer_spec(pool_bd.shape, layer),
                  _layer_spec(pool_scale.shape, layer), _const_spec(pool_norm.shape),
                  _const_spec(cc.shape), _const_spec(ssn.shape), _const_spec(tw.shape),
                  _const_spec(dft.shape)],
        out_specs=(pl.BlockSpec((1, n, D_MODEL), lambda i, j: (i, 0, j)),
                   seq_out, state, seq_out, seq_out, seq_out),
        scratch_shapes=[pltpu.VMEM((nc, MIX, MIX), BF16), pltpu.VMEM((nc, MIX, MIX), BF16),
                        pltpu.VMEM((RET_HEADS, RET_CHUNK, RET_CHUNK), F32),
                        pltpu.VMEM((6, RET_CHUNK, MIX), F32),
                        pltpu.VMEM((MIX // LANES, n, LANES), F32)],
        compiler_params=_params("arbitrary", "arbitrary"),
        name="mixer",
    )(h, wg, pa, pa, pa, pa, rd_lane, s0, avg, conv_w, pool_bd, pool_scale, pool_norm,
      cc, ssn, tw, dft)


def _state_kernel(k_ref, v_ref, rd_ref, s0_ref, sout_ref, sf_s, sb_s, dm_s, dec_s, *, n):
    _retention_states(k_ref, v_ref, rd_ref, s0_ref, sout_ref, sf_s, sb_s, dm_s, dec_s, n=n)


def _final_states(pa, rd_lane, s0, layer):
    b, n, _ = pa.shape
    nc = n // RET_CHUNK
    state = pl.BlockSpec((1, 2, MIX, MIX), lambda i: (i, 0, 0, 0))
    return pl.pallas_call(
        functools.partial(_state_kernel, n=n),
        out_shape=jax.ShapeDtypeStruct((b, 2, MIX, MIX), F32),
        grid=(b,),
        in_specs=[pl.BlockSpec((1, n, MIX), lambda i: (i, 0, 1)),
                  pl.BlockSpec((1, n, MIX), lambda i: (i, 0, 2)),
                  _layer_spec(rd_lane.shape, layer), state],
        out_specs=state,
        scratch_shapes=[pltpu.VMEM((nc, MIX, MIX), BF16), pltpu.VMEM((nc, MIX, MIX), BF16),
                        pltpu.VMEM((RET_HEADS, RET_CHUNK, RET_CHUNK), F32),
                        pltpu.VMEM((6, RET_CHUNK, MIX), F32)],
        compiler_params=_params("arbitrary"),
        name="final_states",
    )(pa, pa, rd_lane, s0)


FFN_SPLITS = (0, 1536, D_FF)


def _post_kernel(b0_ref, b1_ref, b2_ref, b3_ref, pg_ref, x_ref, mod_ref, g_ref,
                 wb_ref, wo_ref, wup_ref, wdn_ref, o_ref):
    mod = mod_ref[0, 0]
    g = g_ref[0]
    merged = None
    for i, br in enumerate((b0_ref, b1_ref, b2_ref, b3_ref)):
        gate = pg_ref[0, :, i * D_MODEL:(i + 1) * D_MODEL].astype(F32)
        t = gate * _dot(br[0], wb_ref[0, i])
        merged = t if merged is None else merged + t
    mix = _dot(merged.astype(BF16), wo_ref[0])
    x1 = x_ref[0] + _rms(mix, g[1:2] * mod[2:3])

    h = (_rms(x1, g[2:3] * (1.0 + mod[4:5])) + mod[3:4]).astype(BF16)
    ffn = None
    for lo, hi in zip(FFN_SPLITS[:-1], FFN_SPLITS[1:]):
        a = _dot(h, wup_ref[0, :, lo:hi])
        u = _dot(h, wup_ref[0, :, D_FF + lo:D_FF + hi])
        t = _dot((_silu(a) * u).astype(BF16), wdn_ref[0, lo:hi, :])
        ffn = t if ffn is None else ffn + t
    o_ref[0] = x1 + _rms(ffn, g[3:4] * mod[5:6])


def _post(branches, pg, x, mod, norm_g, wb, wo, wup, wdn, layer, mod_row, tm):
    b, n, _ = x.shape
    br = pl.BlockSpec((1, tm, MIX), lambda i, j: (i, j, 0))
    return pl.pallas_call(
        _post_kernel,
        out_shape=jax.ShapeDtypeStruct((b, n, D_MODEL), F32),
        grid=(b, n // tm),
        in_specs=[br, br, br, br,
                  pl.BlockSpec((1, tm, PG_WIDTH), lambda i, j: (i, j, 0)),
                  pl.BlockSpec((1, tm, D_MODEL), lambda i, j: (i, j, 0)),
                  _mod_spec(layer, mod_row),
                  _layer_spec(norm_g.shape, layer),
                  _layer_spec(wb.shape, layer),
                  _layer_spec(wo.shape, layer),
                  _layer_spec(wup.shape, layer),
                  _layer_spec(wdn.shape, layer)],
        out_specs=pl.BlockSpec((1, tm, D_MODEL), lambda i, j: (i, j, 0)),
        compiler_params=_params("parallel", "parallel"),
        name="merge_ffn",
    )(*branches, pg, x, mod, norm_g, wb, wo, wup, wdn)


def _rope_tables(n):
    rows = np.repeat(np.arange(n // GRID_W, dtype=np.float64), GRID_W)
    cols = np.tile(np.arange(GRID_W, dtype=np.float64), n // GRID_W)
    lane = np.arange(MIX)
    in_head = lane % RET_HEAD_DIM
    pos = np.where((in_head < RET_HEAD_DIM // 2)[None, :], rows[:, None], cols[:, None])
    quarter = RET_HEAD_DIM // 4
    freq = ROPE_BASE ** (-np.arange(quarter, dtype=np.float64) / quarter)
    ang = pos * freq[lane % quarter][None, :]
    sign = np.where((lane % (2 * quarter)) < quarter, -1.0, 1.0)[None, :]
    return np.cos(ang).astype(np.float32), (np.sin(ang) * sign).astype(np.float32)


def _dft_tables(n):
    m = DFT_BLOCK
    r_cnt = n // m
    k = np.arange(m, dtype=np.int64)
    ang = 2.0 * np.pi * ((k[:, None] * k[None, :]) % m) / m
    dft = np.concatenate([np.cos(ang), np.sin(ang)], axis=1).astype(np.float32)
    angt = 2.0 * np.pi * (np.arange(r_cnt)[:, None] * k[None, :]) / n
    tw = np.stack([np.cos(angt), np.sin(angt)])[..., None].repeat(LANES, axis=-1).astype(np.float32)
    ch = np.arange(FFT_GROUP_DIM, dtype=np.int64)
    angc = 2.0 * np.pi * ((ch[:, None] * ch[None, :]) % FFT_GROUP_DIM) / FFT_GROUP_DIM
    eye = np.eye(FFT_GROUPS)
    cc = np.kron(eye, np.cos(angc)).astype(np.float32)
    ssn = np.kron(eye, -np.sin(angc)).astype(np.float32)
    return dft, tw, cc, ssn


def _pool_norm(n):
    t = np.arange(n)
    cols = []
    for w in POOL_WINDOWS:
        cnt = np.clip(t - w // 2 + w, 0, n) - np.clip(t - w // 2, 0, n)
        inv = 1.0 / cnt
        assert np.all(cnt[EDGE_ROWS:n - EDGE_ROWS] == w)
        col = np.stack([inv[:EDGE_ROWS], np.full(EDGE_ROWS, 1.0 / w), inv[n - EDGE_ROWS:]])
        cols.append(np.repeat(col[:, :, None], POOL_GROUP, axis=2))
    return np.concatenate(cols, axis=2).astype(np.float32)


def _block_diag(w):
    nl, g, d, _ = w.shape
    eye = jnp.eye(g, dtype=w.dtype)
    return (eye[None, :, None, :, None] * w[:, :, :, None, :]).reshape(nl, g * d, g * d)


def kernel(x, c, ctx, c_ctx, w_mod, b_mod, norm_g, w_in, ret_decay, conv_w, pool_w, pool_scale,
           w_branch, w_o, ffn_w_up, ffn_w_down):
    b, n, _ = x.shape
    n_ctx = ctx.shape[1]

    cc_rows = jnp.concatenate(
        [c, c_ctx[None, :], jnp.zeros((MOD_ROWS - b - 1, D_MODEL), F32)], axis=0)
    mod = _modulation(cc_rows, w_mod, b_mod).reshape(DEPTH, MOD_ROWS, 6, D_MODEL)
    ctx_row = b

    cos_l, sin_l = (jnp.asarray(t) for t in _rope_tables(n))
    cos_c, sin_c = jnp.ones((n_ctx, MIX), F32), jnp.zeros((n_ctx, MIX), F32)

    def fourier_tables(length):
        dft, tw, cc, ssn = _dft_tables(length)
        return (jnp.asarray(cc).astype(BF16), jnp.asarray(ssn).astype(BF16), jnp.asarray(tw),
                jnp.asarray(dft).astype(BF16))

    tabs_l, tabs_c = fourier_tables(n), fourier_tables(n_ctx)
    norm_l, norm_c = jnp.asarray(_pool_norm(n)), jnp.asarray(_pool_norm(n_ctx))
    avg = jnp.asarray(np.kron(np.eye(RET_HEADS), np.full((RET_HEAD_DIM, RET_HEAD_DIM),
                                                         1.0 / RET_HEAD_DIM)), F32).astype(BF16)
    s_zero = jnp.zeros((b, 2, MIX, MIX), F32)

    w_in_b = w_in.astype(BF16)
    wg = w_in_b[:, :, PA_WIDTH:]
    wb = w_branch.astype(BF16)
    wo = w_o.astype(BF16)
    wup = ffn_w_up.astype(BF16)
    wdn = ffn_w_down.astype(BF16)
    pool_bd = _block_diag(pool_w).astype(BF16)
    rd_lane = jnp.repeat(ret_decay.astype(F32), RET_HEAD_DIM, axis=2)
    ps = pool_scale.reshape(DEPTH, 1, MIX)

    for l in range(DEPTH):
        need_ctx = l < DEPTH - 1
        if need_ctx:
            h_c, pa_c = _in_proj(ctx, mod, norm_g, w_in_b, cos_c, sin_c, l, ctx_row, tm=n_ctx)
            pg_c, ret_c, s_ctx, conv_c, pool_c, four_c = _mixer(
                h_c, pa_c, wg, rd_lane, s_zero, avg, conv_w, pool_bd, ps, norm_c, *tabs_c, l)
        else:
            _, qkv_c = _in_proj(ctx, mod, norm_g, w_in_b, cos_c, sin_c, l, ctx_row, tm=n_ctx,
                                pa_width=3 * MIX)
            s_ctx = _final_states(qkv_c, rd_lane, s_zero, l)

        h_l, pa_l = _in_proj(x, mod, norm_g, w_in_b, cos_l, sin_l, l, None, tm=1024)
        pg_l, ret_l, _, conv_l, pool_l, four_l = _mixer(
            h_l, pa_l, wg, rd_lane, s_ctx, avg, conv_w, pool_bd, ps, norm_l, *tabs_l, l)
        x = _post((ret_l, conv_l, four_l, pool_l), pg_l, x, mod, norm_g,
                  wb, wo, wup, wdn, l, None, tm=512)
        if need_ctx:
            ctx = _post((ret_c, conv_c, four_c, pool_c), pg_c, ctx, mod,
                        norm_g, wb, wo, wup, wdn, l, ctx_row, tm=n_ctx)
    return x
```

```python
import cmath
import functools
import math

import numpy as np
import jax
import jax.numpy as jnp
from jax import lax
from jax.experimental import pallas as pl
from jax.experimental.pallas import tpu as pltpu

D_MODEL = 1024
DEPTH = 4
GRID_W = 64
MIX = 256
N_BRANCH = 4
RET_HEADS = 4
RET_HEAD_DIM = MIX // RET_HEADS
POOL_WINDOWS = (2, 4, 8, 16)
POOL_GROUP = MIX // len(POOL_WINDOWS)
FFT_GROUPS = 4
FFT_GROUP_DIM = MIX // FFT_GROUPS
D_FF = 2816
PA_WIDTH = 9 * MIX
PG_WIDTH = N_BRANCH * D_MODEL
ROPE_BASE = 10000.0
EPS = 1e-6

LANES = 128
RET_CHUNK = 256
RET_GROUP = 2
DFT_BLOCK = 256
SEQ_PAD = 16
EDGE_ROWS = 8
ROW_CHUNK = 512
GATE_COLS = 256
CONV_ROWS = 128
MOD_ROWS = 16
VMEM_LIMIT = 60 * 1024 * 1024

F32 = jnp.float32
BF16 = jnp.bfloat16


def _dot(a, b):
    return jnp.dot(a, b, preferred_element_type=F32)


def _rms(xf, g):
    ms = jnp.mean(xf * xf, axis=-1, keepdims=True)
    return xf * lax.rsqrt(ms + EPS) * g


def _silu(x):
    return x * jax.nn.sigmoid(x)


def _const_spec(shape):
    zeros = (0,) * len(shape)
    return pl.BlockSpec(shape, lambda *_: zeros, pipeline_mode=pl.Buffered(1))


def _layer_spec(shape, layer):
    tail = (0,) * (len(shape) - 1)
    return pl.BlockSpec((1,) + tuple(shape[1:]), lambda *_: (layer,) + tail,
                        pipeline_mode=pl.Buffered(1))


def _mod_spec(layer, row):
    return pl.BlockSpec((1, 1, 6, D_MODEL),
                        lambda i, j: (layer, i if row is None else row, 0, 0))


def _params(*sem):
    return pltpu.CompilerParams(dimension_semantics=sem, vmem_limit_bytes=VMEM_LIMIT)


def _mod_kernel(c_ref, w_ref, b_ref, o_ref):
    a = _silu(c_ref[...]).astype(BF16)
    o_ref[0] = _dot(a, w_ref[0].astype(BF16)) + b_ref[0]


def _modulation(cc, w_mod, b_mod):
    tn = 1536
    n_out = w_mod.shape[-1]
    return pl.pallas_call(
        _mod_kernel,
        out_shape=jax.ShapeDtypeStruct((DEPTH, MOD_ROWS, n_out), F32),
        grid=(DEPTH, n_out // tn),
        in_specs=[
            pl.BlockSpec((MOD_ROWS, D_MODEL), lambda l, j: (0, 0)),
            pl.BlockSpec((1, D_MODEL, tn), lambda l, j: (l, 0, j)),
            pl.BlockSpec((1, 1, tn), lambda l, j: (l, 0, j)),
        ],
        out_specs=pl.BlockSpec((1, MOD_ROWS, tn), lambda l, j: (l, 0, j)),
        compiler_params=_params("parallel", "parallel"),
        name="modulation",
    )(cc, w_mod, b_mod.reshape(DEPTH, 1, n_out))


def _in_kernel(x_ref, mod_ref, g_ref, w_ref, cos_ref, sin_ref, h_ref, pa_ref):
    mod = mod_ref[0, 0]
    h = _rms(x_ref[0], g_ref[0, 0:1] * (1.0 + mod[1:2])) + mod[0:1]
    hb = h.astype(BF16)
    h_ref[0] = hb
    cos, sin = cos_ref[...], sin_ref[...]
    first_of_pair = (lax.broadcasted_iota(jnp.int32, cos.shape, 1) & 16) == 0

    def rope(t, scale):
        partner = jnp.where(first_of_pair, pltpu.roll(t, MIX - 16, 1), pltpu.roll(t, 16, 1))
        return (t * cos + partner * sin) * scale

    for j in range(0, pa_ref.shape[-1], 3 * MIX):
        res = _dot(hb, w_ref[0, :, j:j + 3 * MIX])
        if j == 0:
            pa_ref[0, :, 0:MIX] = rope(res[:, 0:MIX], RET_HEAD_DIM ** -0.5).astype(BF16)
            pa_ref[0, :, MIX:2 * MIX] = rope(res[:, MIX:2 * MIX], 1.0).astype(BF16)
            pa_ref[0, :, 2 * MIX:3 * MIX] = res[:, 2 * MIX:3 * MIX].astype(BF16)
        else:
            pa_ref[0, :, j:j + 3 * MIX] = res.astype(BF16)


def _in_proj(x, mod, norm_g, w_in, cos, sin, layer, mod_row, tm, pa_width=PA_WIDTH):
    b, n, _ = x.shape
    return pl.pallas_call(
        _in_kernel,
        out_shape=(jax.ShapeDtypeStruct((b, n, D_MODEL), BF16),
                   jax.ShapeDtypeStruct((b, n, pa_width), BF16)),
        grid=(b, n // tm),
        in_specs=[
            pl.BlockSpec((1, tm, D_MODEL), lambda i, j: (i, j, 0)),
            _mod_spec(layer, mod_row),
            _layer_spec(norm_g.shape, layer),
            pl.BlockSpec((1, D_MODEL, pa_width), lambda i, j: (0, 0, 0),
                         pipeline_mode=pl.Buffered(1)),
            pl.BlockSpec((tm, MIX), lambda i, j: (j, 0)),
            pl.BlockSpec((tm, MIX), lambda i, j: (j, 0)),
        ],
        out_specs=(pl.BlockSpec((1, tm, D_MODEL), lambda i, j: (i, j, 0)),
                   pl.BlockSpec((1, tm, pa_width), lambda i, j: (i, j, 0))),
        compiler_params=_params("parallel", "parallel"),
        name="in_proj",
    )(x, mod, norm_g, w_in, cos, sin)


def _retention_states(k_ref, v_ref, rd_ref, s0_ref, sout_ref, sf_s, sb_s, dm_s, dec_s, *, n,
                      fill=lambda: None):
    c_len = RET_CHUNK
    nc = n // c_len

    @pl.when(pl.program_id(0) == 0)
    def _decay_tables():
        rd = rd_ref[0]
        lg = jnp.minimum(rd, 0.0) - jnp.log1p(jnp.exp(-jnp.abs(rd)))
        lgf, lgb = lg[0:1], lg[1:2]
        idx = lax.broadcasted_iota(jnp.int32, (c_len, MIX), 0).astype(F32)
        dec_s[0] = jnp.exp(lgf * (idx + 1.0))
        dec_s[1] = jnp.exp(lgf * (c_len - 1.0 - idx))
        dec_s[2] = jnp.exp(lgb * (c_len - idx))
        dec_s[3] = jnp.exp(lgb * idx)
        dec_s[4] = jnp.exp(lgf * float(c_len)) + 0.0 * idx
        dec_s[5] = jnp.exp(lgb * float(c_len)) + 0.0 * idx
        ii = lax.broadcasted_iota(jnp.int32, (c_len, c_len), 0)
        jj = lax.broadcasted_iota(jnp.int32, (c_len, c_len), 1)
        diff = (ii - jj).astype(F32)
        for h in range(RET_HEADS):
            lo = h * RET_HEAD_DIM
            rate = jnp.where(diff >= 0.0, lgf[:, lo:lo + 1], -lgb[:, lo:lo + 1])
            dm_s[h] = jnp.exp(rate * diff)

    row_head = lax.broadcasted_iota(jnp.int32, (MIX, MIX), 0) // RET_HEAD_DIM
    col_head = lax.broadcasted_iota(jnp.int32, (MIX, MIX), 1) // RET_HEAD_DIM
    block_diag = row_head == col_head
    tn_dims = (((0,), (0,)), ((), ()))

    def advance(state, c, key_decay, chunk_decay):
        rows = pl.ds(c * c_len, c_len)
        kd = (k_ref[0, rows, :].astype(F32) * key_decay).astype(BF16)
        upd = lax.dot_general(kd, v_ref[0, rows, :], tn_dims, preferred_element_type=F32)
        return state * chunk_decay + jnp.where(block_diag, upd, 0.0)

    fwd, bwd = s0_ref[0, 0], s0_ref[0, 1]
    for i in range(nc):
        if i % 2 == 0:
            fill()
        sf_s[i] = fwd.astype(BF16)
        sb_s[nc - 1 - i] = bwd.astype(BF16)
        fwd = advance(fwd, i, dec_s[1], dec_s[4])
        bwd = advance(bwd, nc - 1 - i, dec_s[3], dec_s[5])
    sout_ref[0, 0] = fwd
    sout_ref[0, 1] = bwd


def _retention_step(q_ref, k_ref, v_ref, g_ref, rd_ref, s0_ref, avg_ref, o_ref, sout_ref,
                    sf_s, sb_s, dm_s, dec_s, *, n, fill):
    c_len = RET_CHUNK
    nc = n // c_len
    head = lax.broadcasted_iota(jnp.int32, (c_len, MIX), 1) // RET_HEAD_DIM
    _retention_states(k_ref, v_ref, rd_ref, s0_ref, sout_ref, sf_s, sb_s, dm_s, dec_s, n=n,
                      fill=fill)

    nt_dims = (((1,), (1,)), ((), ()))
    avg = avg_ref[...]
    for c0 in range(0, nc, RET_GROUP):
        group = range(c0, min(c0 + RET_GROUP, nc))
        scores, cross = {}, {}
        for c in group:
            fill()
            rows = pl.ds(c * c_len, c_len)
            q, k = q_ref[0, rows, :], k_ref[0, rows, :]
            qf = q.astype(F32)
            cross[c] = (_dot((qf * dec_s[0]).astype(BF16), sf_s[c])
                        + _dot((qf * dec_s[2]).astype(BF16), sb_s[c]))
            scores[c] = [lax.dot_general(jnp.where(head == h, q, jnp.zeros_like(q)), k, nt_dims,
                                         preferred_element_type=F32) for h in range(RET_HEADS)]
        ys = {}
        for c in group:
            v = v_ref[0, pl.ds(c * c_len, c_len), :]
            probs = [(scores[c][h] * dm_s[h]).astype(BF16) for h in range(RET_HEADS)]
            vals = [jnp.where(head == h, v, jnp.zeros_like(v)) for h in range(RET_HEADS)]
            ys[c] = cross[c] + _dot(jnp.concatenate(probs, axis=1), jnp.concatenate(vals, axis=0))
        for c in group:
            if c % 2:
                fill()
            rows = pl.ds(c * c_len, c_len)
            d = ys[c] - _dot(ys[c].astype(BF16), avg)
            yn = d * lax.rsqrt(_dot((d * d).astype(BF16), avg) + EPS)
            o_ref[0, rows, :] = (yn * _silu(g_ref[0, rows, :].astype(F32))).astype(BF16)


def _padded_rows(ref, r0, rows, n):
    zeros = jnp.zeros((SEQ_PAD, MIX), ref.dtype)
    top = zeros if r0 == 0 else ref[0, r0 - SEQ_PAD:r0, :]
    bot = zeros if r0 + rows == n else ref[0, r0 + rows:r0 + rows + SEQ_PAD, :]
    return jnp.concatenate([top, ref[0, r0:r0 + rows, :], bot], axis=0).astype(F32)


def _shift_rows(t, s):
    return pltpu.roll(t, s % t.shape[0], 0)


def _conv_step(cb_ref, cg_ref, cx_ref, cw_ref, conv_ref, *, n, fill):
    rows = min(n, CONV_ROWS)
    main = slice(SEQ_PAD, SEQ_PAD + rows)
    cw = cw_ref[0]
    for r0 in range(0, n, rows):
        fill()
        u = _padded_rows(cg_ref, r0, rows, n) * _padded_rows(cx_ref, r0, rows, n)
        conv = _shift_rows(u, 1) * cw[0:1] + u * cw[1:2] + _shift_rows(u, -1) * cw[2:3]
        conv_ref[0, r0:r0 + rows, :] = (
            cb_ref[0, r0:r0 + rows, :].astype(F32) * conv[main]).astype(BF16)


def _pool_step(pu_ref, pw_ref, ps_ref, norm_ref, pool_ref, *, n, fill):
    rows = min(n, CONV_ROWS)
    main = slice(SEQ_PAD, SEQ_PAD + rows)
    in_first_group = lax.broadcasted_iota(jnp.int32, (rows, LANES), 1) < POOL_GROUP
    for r0 in range(0, n, rows):
        fill()
        e_all = _padded_rows(pu_ref, r0, rows, n)
        halves = []
        for half in range(MIX // LANES):
            e = e_all[:, half * LANES:(half + 1) * LANES]
            c = e + _shift_rows(e, 1)
            sums = [c]
            for w in POOL_WINDOWS[1:2 * half + 2]:
                c = _shift_rows(c, w // 4) + _shift_rows(c, -(w // 4))
                sums.append(c)
            halves.append(jnp.where(in_first_group, sums[-2][main], sums[-1][main]))
        win = jnp.concatenate(halves, axis=1)
        mean = win * norm_ref[1, 0:1]
        if r0 == 0:
            mean = jnp.concatenate([win[:EDGE_ROWS] * norm_ref[0], mean[EDGE_ROWS:]], axis=0)
        if r0 + rows == n:
            mean = jnp.concatenate([mean[:-EDGE_ROWS], win[-EDGE_ROWS:] * norm_ref[2]], axis=0)
        pooled = (mean - e_all[main]).astype(BF16)
        pool_ref[0, r0:r0 + rows, :] = (_dot(pooled, pw_ref[0]) * ps_ref[0]).astype(BF16)


def _times_const(z, w):
    re, im = z

    def lin(a, ca, b, cb):
        acc = None
        for t, coef in ((a, ca), (b, cb)):
            if abs(coef) < 1e-12:
                continue
            unit = abs(abs(coef) - 1.0) < 1e-12
            term = t if unit else t * abs(coef)
            if acc is None:
                acc = term if coef > 0 else -term
            else:
                acc = acc + term if coef > 0 else acc - term
        return acc

    return lin(re, w.real, im, -w.imag), lin(re, w.imag, im, w.real)


def _dft_blocks(xs):
    count = len(xs)
    if count == 1:
        return xs
    even, odd = _dft_blocks(xs[0::2]), _dft_blocks(xs[1::2])
    out = [None] * count
    for k in range(count // 2):
        t = _times_const(odd[k], cmath.exp(-2j * math.pi * k / count))
        out[k] = (even[k][0] + t[0], even[k][1] + t[1])
        out[k + count // 2] = (even[k][0] - t[0], even[k][1] - t[1])
    return out


def _fourier_step(u_ref, cc_ref, ssn_ref, tw_ref, dft_ref, o_ref, spec_s, *, n, fill):
    m = DFT_BLOCK
    r_cnt = n // m
    scale = 1.0 / math.sqrt(n * FFT_GROUP_DIM)
    blocks = []
    for r in range(r_cnt):
        u = u_ref[0, r * m:(r + 1) * m, :]
        blocks.append((_dot(u, cc_ref[...]), _dot(u, ssn_ref[...])))
        fill()
    spectra = _dft_blocks(blocks)
    for k1 in range(r_cnt):
        fill()
        re, im = spectra[k1]
        if k1 > 0:
            tc = jnp.concatenate([tw_ref[0, k1]] * (MIX // LANES), axis=1)
            ts = jnp.concatenate([tw_ref[1, k1]] * (MIX // LANES), axis=1)
            re, im = re * tc + im * ts, im * tc - re * ts
        rhs = jnp.concatenate([re.astype(BF16), im.astype(BF16)], axis=0)
        block = _dot(dft_ref[...], rhs) * scale
        for half in range(MIX // LANES):
            spec_s[half, pl.ds(k1, m, stride=r_cnt), :] = block[:, half * LANES:(half + 1) * LANES]
    o_ref[0] = jnp.concatenate([spec_s[half] for half in range(MIX // LANES)],
                               axis=1).astype(BF16)


def _mixer_kernel(h_ref, wg_ref, a_ref, b_ref, c_ref, d_ref, rd_ref, s0_ref, avg_ref,
                  cw_ref, pw_ref, ps_ref, norm_ref, cc_ref, ssn_ref, tw_ref, dft_ref,
                  pg_ref, ret_ref, sout_ref, conv_ref, pool_ref, four_ref,
                  sf_s, sb_s, dm_s, dec_s, spec_s, *, n):
    step = pl.program_id(1)
    rows = min(n, ROW_CHUNK)

    def gate_filler(j):
        todo = [(r0, c0) for r0 in range(0, n, rows) for c0 in range(0, D_MODEL, GATE_COLS)]
        calls_per_piece = max(1, (n // CONV_ROWS) // len(todo))
        pieces = iter(todo)
        calls = iter(range(n))

        def emit():
            r0, c0 = next(pieces, (None, None))
            if r0 is not None:
                w = wg_ref[0, :, j * D_MODEL + c0:j * D_MODEL + c0 + GATE_COLS]
                gate = jax.nn.sigmoid(_dot(h_ref[0, r0:r0 + rows, :], w))
                pg_ref[0, r0:r0 + rows, c0:c0 + GATE_COLS] = gate.astype(BF16)

        def fill():
            if next(calls) % calls_per_piece == 0:
                emit()

        def drain():
            for _ in todo:
                emit()

        return fill, drain

    @pl.when(step == 0)
    def _():
        fill, drain = gate_filler(0)
        _retention_step(a_ref, b_ref, c_ref, d_ref, rd_ref, s0_ref, avg_ref, ret_ref, sout_ref,
                        sf_s, sb_s, dm_s, dec_s, n=n, fill=fill)
        drain()

    @pl.when(step == 1)
    def _():
        fill, drain = gate_filler(1)
        _conv_step(a_ref, b_ref, c_ref, cw_ref, conv_ref, n=n, fill=fill)
        drain()

    @pl.when(step == 2)
    def _():
        fill, drain = gate_filler(2)
        _fourier_step(a_ref, cc_ref, ssn_ref, tw_ref, dft_ref, four_ref, spec_s, n=n, fill=fill)
        drain()

    @pl.when(step == 3)
    def _():
        fill, drain = gate_filler(3)
        _pool_step(d_ref, pw_ref, ps_ref, norm_ref, pool_ref, n=n, fill=fill)
        drain()


_SLOT_COLUMNS = ((0, 4, 7, 7), (1, 5, 5, 5), (2, 6, 6, 6), (3, 3, 3, 8))


def _mixer(h, pa, wg, rd_lane, s0, avg, conv_w, pool_bd, pool_scale, pool_norm,
           cc, ssn, tw, dft, layer):
    b, n, _ = pa.shape
    nc = n // RET_CHUNK

    def slot(cols):
        def index(i, j):
            col = jnp.where(j == 0, cols[0],
                            jnp.where(j == 1, cols[1], jnp.where(j == 2, cols[2], cols[3])))
            return (i, 0, col)
        return pl.BlockSpec((1, n, MIX), index)

    seq_out = pl.BlockSpec((1, n, MIX), lambda i, j: (i, 0, 0))
    state = pl.BlockSpec((1, 2, MIX, MIX), lambda i, j: (i, 0, 0, 0))
    return pl.pallas_call(
        functools.partial(_mixer_kernel, n=n),
        out_shape=(jax.ShapeDtypeStruct((b, n, PG_WIDTH), BF16),
                   jax.ShapeDtypeStruct((b, n, MIX), BF16),
                   jax.ShapeDtypeStruct((b, 2, MIX, MIX), F32),
                   jax.ShapeDtypeStruct((b, n, MIX), BF16),
                   jax.ShapeDtypeStruct((b, n, MIX), BF16),
                   jax.ShapeDtypeStruct((b, n, MIX), BF16)),
        grid=(b, 4),
        in_specs=[pl.BlockSpec((1, n, D_MODEL), lambda i, j: (i, 0, 0)),
                  _layer_spec(wg.shape, 0),
                  slot(_SLOT_COLUMNS[0]), slot(_SLOT_COLUMNS[1]),
                  slot(_SLOT_COLUMNS[2]), slot(_SLOT_COLUMNS[3]),
                  _layer_spec(rd_lane.shape, layer), state, _const_spec(avg.shape),
                  _layer_spec(conv_w.shape, layer), _layer_spec(pool_bd.shape, layer),
                  _layer_spec(pool_scale.shape, layer), _const_spec(pool_norm.shape),
                  _const_spec(cc.shape), _const_spec(ssn.shape), _const_spec(tw.shape),
                  _const_spec(dft.shape)],
        out_specs=(pl.BlockSpec((1, n, D_MODEL), lambda i, j: (i, 0, j)),
                   seq_out, state, seq_out, seq_out, seq_out),
        scratch_shapes=[pltpu.VMEM((nc, MIX, MIX), BF16), pltpu.VMEM((nc, MIX, MIX), BF16),
                        pltpu.VMEM((RET_HEADS, RET_CHUNK, RET_CHUNK), F32),
                        pltpu.VMEM((6, RET_CHUNK, MIX), F32),
                        pltpu.VMEM((MIX // LANES, n, LANES), F32)],
        compiler_params=_params("arbitrary", "arbitrary"),
        name="mixer",
    )(h, wg, pa, pa, pa, pa, rd_lane, s0, avg, conv_w, pool_bd, pool_scale, pool_norm,
      cc, ssn, tw, dft)


def _state_kernel(k_ref, v_ref, rd_ref, s0_ref, sout_ref, sf_s, sb_s, dm_s, dec_s, *, n):
    _retention_states(k_ref, v_ref, rd_ref, s0_ref, sout_ref, sf_s, sb_s, dm_s, dec_s, n=n)


def _final_states(pa, rd_lane, s0, layer):
    b, n, _ = pa.shape
    nc = n // RET_CHUNK
    state = pl.BlockSpec((1, 2, MIX, MIX), lambda i: (i, 0, 0, 0))
    return pl.pallas_call(
        functools.partial(_state_kernel, n=n),
        out_shape=jax.ShapeDtypeStruct((b, 2, MIX, MIX), F32),
        grid=(b,),
        in_specs=[pl.BlockSpec((1, n, MIX), lambda i: (i, 0, 1)),
                  pl.BlockSpec((1, n, MIX), lambda i: (i, 0, 2)),
                  _layer_spec(rd_lane.shape, layer), state],
        out_specs=state,
        scratch_shapes=[pltpu.VMEM((nc, MIX, MIX), BF16), pltpu.VMEM((nc, MIX, MIX), BF16),
                        pltpu.VMEM((RET_HEADS, RET_CHUNK, RET_CHUNK), F32),
                        pltpu.VMEM((6, RET_CHUNK, MIX), F32)],
        compiler_params=_params("arbitrary"),
        name="final_states",
    )(pa, pa, rd_lane, s0)


FFN_SPLITS = (0, 768, 1536, 2304, D_FF)
POST_ROWS = 256


def _post_kernel(b0_ref, b1_ref, b2_ref, b3_ref, pg_ref, x_ref, mod_ref, g_ref,
                 wb_ref, wo_ref, wup_ref, wdn_ref, *rest):
    if len(rest) == 1:
        o_ref, = rest
        cast_jobs = []
    else:
        win_f, wb_f, wo_f, wup_f, wdn_f, o_ref, wa_o, wg_o, wb_o, wo_o, wup_o, wdn_o = rest

        def cast_w_in():
            wa_o[0] = win_f[0, :, :PA_WIDTH].astype(BF16)
            wg_o[0] = win_f[0, :, PA_WIDTH:].astype(BF16)

        def cast_rest():
            for src, dst in ((wb_f, wb_o), (wo_f, wo_o), (wup_f, wup_o), (wdn_f, wdn_o)):
                dst[0] = src[0].astype(BF16)

        cast_jobs = [cast_w_in, cast_rest]
    mod = mod_ref[0, 0]
    g = g_ref[0]

    def sub_tile(rows):
        merged = None
        for i, br in enumerate((b0_ref, b1_ref, b2_ref, b3_ref)):
            gate = pg_ref[0, rows, i * D_MODEL:(i + 1) * D_MODEL].astype(F32)
            t = gate * _dot(br[0, rows, :], wb_ref[0, i])
            merged = t if merged is None else merged + t
            if i % 2:
                yield "mix"
        mix = _dot(merged.astype(BF16), wo_ref[0])
        x1 = x_ref[0, rows, :] + _rms(mix, g[1:2] * mod[2:3])
        yield "mix"
        h = (_rms(x1, g[2:3] * (1.0 + mod[4:5])) + mod[3:4]).astype(BF16)
        yield "mix"
        ffn = None
        for lo, hi in zip(FFN_SPLITS[:-1], FFN_SPLITS[1:]):
            a = _dot(h, wup_ref[0, :, lo:hi])
            u = _dot(h, wup_ref[0, :, D_FF + lo:D_FF + hi])
            t = _dot((_silu(a) * u).astype(BF16), wdn_ref[0, lo:hi, :])
            ffn = t if ffn is None else ffn + t
            yield "ffn"
        o_ref[0, rows, :] = x1 + _rms(ffn, g[3:4] * mod[5:6])
        yield "done"

    tm = o_ref.shape[1]
    tiles = [sub_tile(pl.ds(r0, POST_ROWS)) for r0 in range(0, tm, POST_ROWS)]
    mix_pieces = ffn_chunks = len(FFN_SPLITS) - 1
    for _ in range(mix_pieces):
        next(tiles[0])
    for idx, cur in enumerate(tiles):
        for chunk in range(ffn_chunks):
            next(cur)
            if chunk == 0 and idx > 0:
                next(tiles[idx - 1])
            if idx + 1 < len(tiles):
                next(tiles[idx + 1])
            elif chunk >= 1 and cast_jobs:
                cast_jobs.pop()()
    next(tiles[-1])


def _post(branches, pg, x, mod, norm_g, weights, layer, mod_row, tm, cast_next=None):
    b, n, _ = x.shape
    tiles = n // tm
    br = pl.BlockSpec((1, tm, MIX), lambda i, j: (i, j, 0))
    in_specs = [br, br, br, br,
                pl.BlockSpec((1, tm, PG_WIDTH), lambda i, j: (i, j, 0)),
                pl.BlockSpec((1, tm, D_MODEL), lambda i, j: (i, j, 0)),
                _mod_spec(layer, mod_row),
                _layer_spec(norm_g.shape, layer)]
    in_specs += [_layer_spec(w.shape, 0) for w in weights]
    out_shape = [jax.ShapeDtypeStruct((b, n, D_MODEL), F32)]
    out_specs = [pl.BlockSpec((1, tm, D_MODEL), lambda i, j: (i, j, 0))]
    operands = [*branches, pg, x, mod, norm_g, *weights]
    if cast_next is not None:
        steps = b * tiles

        def slab_specs(rows, width, share):
            height = rows * share // steps
            return (pl.BlockSpec((1, height, width),
                                 lambda i, j: (layer + 1, (i * tiles + j) // share, 0)),
                    pl.BlockSpec((1, height, width), lambda i, j: (0, (i * tiles + j) // share, 0)))

        w_in, w_branch, w_o, w_up, w_down = cast_next
        sources = (w_in, w_branch.reshape(DEPTH, N_BRANCH * MIX, D_MODEL), w_o, w_up, w_down)
        shares = (1, 1, 1, 1, 2)
        for src, share in zip(sources, shares):
            src_spec, dst_spec = slab_specs(src.shape[1], src.shape[2], share)
            in_specs.append(src_spec)
            operands.append(src)
            if src is w_in:
                height = src_spec.block_shape[1]
                for lo, width in ((0, PA_WIDTH), (PA_WIDTH, PG_WIDTH)):
                    out_shape.append(jax.ShapeDtypeStruct((1, src.shape[1], width), BF16))
                    out_specs.append(pl.BlockSpec((1, height, width),
                                                  lambda i, j: (0, i * tiles + j, 0)))
            else:
                out_shape.append(jax.ShapeDtypeStruct((1,) + src.shape[1:], BF16))
                out_specs.append(dst_spec)
    out = pl.pallas_call(
        _post_kernel,
        out_shape=tuple(out_shape),
        grid=(b, tiles),
        in_specs=in_specs,
        out_specs=tuple(out_specs),
        compiler_params=_params("arbitrary", "arbitrary"),
        name="merge_ffn",
    )(*operands)
    if cast_next is None:
        return out[0]
    wa, wg, wb, wo, wup, wdn = out[1:]
    return out[0], (wa, wg, wb.reshape(1, N_BRANCH, MIX, D_MODEL), wo, wup, wdn)


def _rope_tables(n):
    rows = np.repeat(np.arange(n // GRID_W, dtype=np.float64), GRID_W)
    cols = np.tile(np.arange(GRID_W, dtype=np.float64), n // GRID_W)
    lane = np.arange(MIX)
    in_head = lane % RET_HEAD_DIM
    pos = np.where((in_head < RET_HEAD_DIM // 2)[None, :], rows[:, None], cols[:, None])
    quarter = RET_HEAD_DIM // 4
    freq = ROPE_BASE ** (-np.arange(quarter, dtype=np.float64) / quarter)
    ang = pos * freq[lane % quarter][None, :]
    sign = np.where((lane % (2 * quarter)) < quarter, -1.0, 1.0)[None, :]
    return np.cos(ang).astype(np.float32), (np.sin(ang) * sign).astype(np.float32)


def _dft_tables(n):
    m = DFT_BLOCK
    r_cnt = n // m
    k = np.arange(m, dtype=np.int64)
    ang = 2.0 * np.pi * ((k[:, None] * k[None, :]) % m) / m
    dft = np.concatenate([np.cos(ang), np.sin(ang)], axis=1).astype(np.float32)
    angt = 2.0 * np.pi * (np.arange(r_cnt)[:, None] * k[None, :]) / n
    tw = np.stack([np.cos(angt), np.sin(angt)])[..., None].repeat(LANES, axis=-1).astype(np.float32)
    ch = np.arange(FFT_GROUP_DIM, dtype=np.int64)
    angc = 2.0 * np.pi * ((ch[:, None] * ch[None, :]) % FFT_GROUP_DIM) / FFT_GROUP_DIM
    eye = np.eye(FFT_GROUPS)
    cc = np.kron(eye, np.cos(angc)).astype(np.float32)
    ssn = np.kron(eye, -np.sin(angc)).astype(np.float32)
    return dft, tw, cc, ssn


def _pool_norm(n):
    t = np.arange(n)
    cols = []
    for w in POOL_WINDOWS:
        cnt = np.clip(t - w // 2 + w, 0, n) - np.clip(t - w // 2, 0, n)
        inv = 1.0 / cnt
        assert np.all(cnt[EDGE_ROWS:n - EDGE_ROWS] == w)
        col = np.stack([inv[:EDGE_ROWS], np.full(EDGE_ROWS, 1.0 / w), inv[n - EDGE_ROWS:]])
        cols.append(np.repeat(col[:, :, None], POOL_GROUP, axis=2))
    return np.concatenate(cols, axis=2).astype(np.float32)


def _block_diag(w):
    nl, g, d, _ = w.shape
    eye = jnp.eye(g, dtype=w.dtype)
    return (eye[None, :, None, :, None] * w[:, :, :, None, :]).reshape(nl, g * d, g * d)


def kernel(x, c, ctx, c_ctx, w_mod, b_mod, norm_g, w_in, ret_decay, conv_w, pool_w, pool_scale,
           w_branch, w_o, ffn_w_up, ffn_w_down):
    b, n, _ = x.shape
    n_ctx = ctx.shape[1]

    cc_rows = jnp.concatenate(
        [c, c_ctx[None, :], jnp.zeros((MOD_ROWS - b - 1, D_MODEL), F32)], axis=0)
    mod = _modulation(cc_rows, w_mod, b_mod).reshape(DEPTH, MOD_ROWS, 6, D_MODEL)
    ctx_row = b

    cos_l, sin_l = (jnp.asarray(t) for t in _rope_tables(n))
    cos_c, sin_c = jnp.ones((n_ctx, MIX), F32), jnp.zeros((n_ctx, MIX), F32)

    def fourier_tables(length):
        dft, tw, cc, ssn = _dft_tables(length)
        return (jnp.asarray(cc).astype(BF16), jnp.asarray(ssn).astype(BF16), jnp.asarray(tw),
                jnp.asarray(dft).astype(BF16))

    tabs_l, tabs_c = fourier_tables(n), fourier_tables(n_ctx)
    norm_l, norm_c = jnp.asarray(_pool_norm(n)), jnp.asarray(_pool_norm(n_ctx))
    avg = jnp.asarray(np.kron(np.eye(RET_HEADS), np.full((RET_HEAD_DIM, RET_HEAD_DIM),
                                                         1.0 / RET_HEAD_DIM)), F32).astype(BF16)
    s_zero = jnp.zeros((b, 2, MIX, MIX), F32)

    w_layer = (w_in[:1, :, :PA_WIDTH].astype(BF16), w_in[:1, :, PA_WIDTH:].astype(BF16),
               w_branch[:1].astype(BF16), w_o[:1].astype(BF16),
               ffn_w_up[:1].astype(BF16), ffn_w_down[:1].astype(BF16))
    w_stacked = (w_in, w_branch, w_o, ffn_w_up, ffn_w_down)
    pool_bd = _block_diag(pool_w).astype(BF16)
    rd_lane = jnp.repeat(ret_decay.astype(F32), RET_HEAD_DIM, axis=2)
    ps = pool_scale.reshape(DEPTH, 1, MIX)

    for l in range(DEPTH):
        need_ctx = l < DEPTH - 1
        wa, wg = w_layer[:2]
        if need_ctx:
            h_c, pa_c = _in_proj(ctx, mod, norm_g, wa, cos_c, sin_c, l, ctx_row, tm=n_ctx)
            pg_c, ret_c, s_ctx, conv_c, pool_c, four_c = _mixer(
                h_c, pa_c, wg, rd_lane, s_zero, avg, conv_w, pool_bd, ps, norm_c, *tabs_c, l)
        else:
            _, qkv_c = _in_proj(ctx, mod, norm_g, wa, cos_c, sin_c, l, ctx_row, tm=n_ctx,
                                pa_width=3 * MIX)
            s_ctx = _final_states(qkv_c, rd_lane, s_zero, l)

        h_l, pa_l = _in_proj(x, mod, norm_g, wa, cos_l, sin_l, l, None, tm=1024)
        pg_l, ret_l, _, conv_l, pool_l, four_l = _mixer(
            h_l, pa_l, wg, rd_lane, s_ctx, avg, conv_w, pool_bd, ps, norm_l, *tabs_l, l)
        if need_ctx:
            ctx = _post((ret_c, conv_c, four_c, pool_c), pg_c, ctx, mod, norm_g, w_layer[2:],
                        l, ctx_row, tm=n_ctx)
            x, w_layer = _post((ret_l, conv_l, four_l, pool_l), pg_l, x, mod, norm_g, w_layer[2:],
                               l, None, tm=512, cast_next=w_stacked)
        else:
            x = _post((ret_l, conv_l, four_l, pool_l), pg_l, x, mod, norm_g, w_layer[2:],
                      l, None, tm=512)
    return x
```

```python
import cmath
import functools
import math
from typing import NamedTuple

import numpy as np
import jax
import jax.numpy as jnp
from jax import lax
from jax.experimental import pallas as pl
from jax.experimental.pallas import tpu as pltpu

D_MODEL = 1024
DEPTH = 4
GRID_W = 64
MIX = 256
N_BRANCH = 4
RET_HEADS = 4
RET_HEAD_DIM = MIX // RET_HEADS
POOL_WINDOWS = (2, 4, 8, 16)
POOL_GROUP = MIX // len(POOL_WINDOWS)
FFT_GROUPS = 4
FFT_GROUP_DIM = MIX // FFT_GROUPS
D_FF = 2816
PA_WIDTH = 9 * MIX
PG_WIDTH = N_BRANCH * D_MODEL
ROPE_BASE = 10000.0
EPS = 1e-6

LANES = 128
RET_CHUNK = 256
RET_GROUP = 4
DFT_BLOCK = 256
SEQ_PAD = 16
EDGE_ROWS = 8
GATE_ROWS_MXU_STEP = 512
GATE_ROWS_VPU_STEP = 256
GATE_COLS = 256
CONV_ROWS = 64
MOD_ROWS = 16
VMEM_LIMIT = 60 * 1024 * 1024

F32 = jnp.float32
BF16 = jnp.bfloat16


def _dot(a, b):
    return jnp.dot(a, b, preferred_element_type=F32)


def _rms(xf, g):
    ms = jnp.mean(xf * xf, axis=-1, keepdims=True)
    return xf * lax.rsqrt(ms + EPS) * g


def _sigmoid(x):
    return 0.5 * jnp.tanh(0.5 * x) + 0.5


def _silu(x):
    return x * _sigmoid(x)


def _const_spec(shape):
    zeros = (0,) * len(shape)
    return pl.BlockSpec(shape, lambda *_: zeros, pipeline_mode=pl.Buffered(1))


def _layer_spec(shape, layer):
    tail = (0,) * (len(shape) - 1)
    return pl.BlockSpec((1,) + tuple(shape[1:]), lambda *_: (layer,) + tail,
                        pipeline_mode=pl.Buffered(1))


def _mod_spec(layer, row):
    return pl.BlockSpec((1, 1, 6, D_MODEL),
                        lambda i, j: (layer, i if row is None else row, 0, 0))


def _params(*sem):
    return pltpu.CompilerParams(dimension_semantics=sem, vmem_limit_bytes=VMEM_LIMIT)


def _mod_kernel(c_ref, w_ref, b_ref, o_ref):
    a = _silu(c_ref[...]).astype(BF16)
    o_ref[0] = _dot(a, w_ref[0].astype(BF16)) + b_ref[0]


def _modulation(cc, w_mod, b_mod):
    tn = 1536
    n_out = w_mod.shape[-1]
    return pl.pallas_call(
        _mod_kernel,
        out_shape=jax.ShapeDtypeStruct((DEPTH, MOD_ROWS, n_out), F32),
        grid=(DEPTH, n_out // tn),
        in_specs=[
            pl.BlockSpec((MOD_ROWS, D_MODEL), lambda l, j: (0, 0)),
            pl.BlockSpec((1, D_MODEL, tn), lambda l, j: (l, 0, j)),
            pl.BlockSpec((1, 1, tn), lambda l, j: (l, 0, j)),
        ],
        out_specs=pl.BlockSpec((1, MOD_ROWS, tn), lambda l, j: (l, 0, j)),
        compiler_params=_params("parallel", "parallel"),
        name="modulation",
    )(cc, w_mod, b_mod.reshape(DEPTH, 1, n_out))


def _in_kernel(x_ref, mod_ref, g_ref, w_ref, cos_ref, sin_ref, h_ref, pa_ref):
    mod = mod_ref[0, 0]
    h = _rms(x_ref[0], g_ref[0, 0:1] * (1.0 + mod[1:2])) + mod[0:1]
    hb = h.astype(BF16)
    h_ref[0] = hb
    cos, sin = cos_ref[...], sin_ref[...]
    first_of_pair = (lax.broadcasted_iota(jnp.int32, cos.shape, 1) & 16) == 0

    def rope(t, scale):
        partner = jnp.where(first_of_pair, pltpu.roll(t, MIX - 16, 1), pltpu.roll(t, 16, 1))
        return (t * cos + partner * sin) * scale

    for j in range(0, pa_ref.shape[-1], 3 * MIX):
        res = _dot(hb, w_ref[0, :, j:j + 3 * MIX])
        if j == 0:
            pa_ref[0, :, 0:MIX] = rope(res[:, 0:MIX], RET_HEAD_DIM ** -0.5).astype(BF16)
            pa_ref[0, :, MIX:2 * MIX] = rope(res[:, MIX:2 * MIX], 1.0).astype(BF16)
            pa_ref[0, :, 2 * MIX:3 * MIX] = res[:, 2 * MIX:3 * MIX].astype(BF16)
        else:
            pa_ref[0, :, j:j + 3 * MIX] = res.astype(BF16)


def _in_proj(x, mod, norm_g, w_in, cos, sin, layer, mod_row, tm, pa_width=PA_WIDTH):
    b, n, _ = x.shape
    return pl.pallas_call(
        _in_kernel,
        out_shape=(jax.ShapeDtypeStruct((b, n, D_MODEL), BF16),
                   jax.ShapeDtypeStruct((b, n, pa_width), BF16)),
        grid=(b, n // tm),
        in_specs=[
            pl.BlockSpec((1, tm, D_MODEL), lambda i, j: (i, j, 0)),
            _mod_spec(layer, mod_row),
            _layer_spec(norm_g.shape, layer),
            pl.BlockSpec((1, D_MODEL, pa_width), lambda i, j: (0, 0, 0),
                         pipeline_mode=pl.Buffered(1)),
            pl.BlockSpec((tm, MIX), lambda i, j: (j, 0)),
            pl.BlockSpec((tm, MIX), lambda i, j: (j, 0)),
        ],
        out_specs=(pl.BlockSpec((1, tm, D_MODEL), lambda i, j: (i, j, 0)),
                   pl.BlockSpec((1, tm, pa_width), lambda i, j: (i, j, 0))),
        compiler_params=_params("parallel", "parallel"),
        name="in_proj",
    )(x, mod, norm_g, w_in, cos, sin)


def _retention_states(k_ref, v_ref, rd_ref, s0_ref, sout_ref, sf_s, sb_s, dm_s, dec_s, *, n,
                      fill=lambda: None):
    c_len = RET_CHUNK
    nc = n // c_len

    @pl.when(pl.program_id(0) == 0)
    def _decay_tables():
        rd = rd_ref[0]
        lg = jnp.minimum(rd, 0.0) - jnp.log1p(jnp.exp(-jnp.abs(rd)))
        lgf, lgb = lg[0:1], lg[1:2]
        idx = lax.broadcasted_iota(jnp.int32, (c_len, MIX), 0).astype(F32)
        dec_s[0] = jnp.exp(lgf * (idx + 1.0))
        dec_s[1] = jnp.exp(lgf * (c_len - 1.0 - idx))
        dec_s[2] = jnp.exp(lgb * (c_len - idx))
        dec_s[3] = jnp.exp(lgb * idx)
        dec_s[4] = jnp.exp(lgf * float(c_len)) + 0.0 * idx
        dec_s[5] = jnp.exp(lgb * float(c_len)) + 0.0 * idx
        ii = lax.broadcasted_iota(jnp.int32, (c_len, c_len), 0)
        jj = lax.broadcasted_iota(jnp.int32, (c_len, c_len), 1)
        diff = (ii - jj).astype(F32)
        for h in range(RET_HEADS):
            lo = h * RET_HEAD_DIM
            rate = jnp.where(diff >= 0.0, lgf[:, lo:lo + 1], -lgb[:, lo:lo + 1])
            dm_s[h] = jnp.exp(rate * diff)

    row_head = lax.broadcasted_iota(jnp.int32, (MIX, MIX), 0) // RET_HEAD_DIM
    col_head = lax.broadcasted_iota(jnp.int32, (MIX, MIX), 1) // RET_HEAD_DIM
    block_diag = row_head == col_head
    tn_dims = (((0,), (0,)), ((), ()))

    def advance(state, c, key_decay, chunk_decay):
        rows = pl.ds(c * c_len, c_len)
        kd = (k_ref[0, rows, :].astype(F32) * key_decay).astype(BF16)
        upd = lax.dot_general(kd, v_ref[0, rows, :], tn_dims, preferred_element_type=F32)
        return state * chunk_decay + jnp.where(block_diag, upd, 0.0)

    fwd, bwd = s0_ref[0, 0], s0_ref[0, 1]
    for i in range(nc):
        fill()
        sf_s[i] = fwd.astype(BF16)
        sb_s[nc - 1 - i] = bwd.astype(BF16)
        fwd = advance(fwd, i, dec_s[1], dec_s[4])
        bwd = advance(bwd, nc - 1 - i, dec_s[3], dec_s[5])
    sout_ref[0, 0] = fwd
    sout_ref[0, 1] = bwd


def _retention_step(q_ref, k_ref, v_ref, g_ref, rd_ref, s0_ref, avg_ref, o_ref, sout_ref,
                    sf_s, sb_s, dm_s, dec_s, *, n, fill):
    c_len = RET_CHUNK
    nc = n // c_len
    head = lax.broadcasted_iota(jnp.int32, (c_len, MIX), 1) // RET_HEAD_DIM
    _retention_states(k_ref, v_ref, rd_ref, s0_ref, sout_ref, sf_s, sb_s, dm_s, dec_s, n=n,
                      fill=fill)

    nt_dims = (((1,), (1,)), ((), ()))
    avg = avg_ref[...]
    for c0 in range(0, nc, RET_GROUP):
        group = range(c0, min(c0 + RET_GROUP, nc))
        scores, cross = {}, {}
        for c in group:
            fill()
            rows = pl.ds(c * c_len, c_len)
            q, k = q_ref[0, rows, :], k_ref[0, rows, :]
            qf = q.astype(F32)
            cross[c] = (_dot((qf * dec_s[0]).astype(BF16), sf_s[c])
                        + _dot((qf * dec_s[2]).astype(BF16), sb_s[c]))
            fill()
            scores[c] = [lax.dot_general(jnp.where(head == h, q, jnp.zeros_like(q)), k, nt_dims,
                                         preferred_element_type=F32) for h in range(RET_HEADS)]
        ys = {}
        for c in group:
            v = v_ref[0, pl.ds(c * c_len, c_len), :]
            probs = [(scores[c][h] * dm_s[h]).astype(BF16) for h in range(RET_HEADS)]
            vals = [jnp.where(head == h, v, jnp.zeros_like(v)) for h in range(RET_HEADS)]
            ys[c] = cross[c] + _dot(jnp.concatenate(probs, axis=1), jnp.concatenate(vals, axis=0))
        for c in group:
            fill()
            rows = pl.ds(c * c_len, c_len)
            d = ys[c] - _dot(ys[c].astype(BF16), avg)
            yn = d * lax.rsqrt(_dot((d * d).astype(BF16), avg) + EPS)
            o_ref[0, rows, :] = (yn * _silu(g_ref[0, rows, :].astype(F32))).astype(BF16)


def _padded_rows(ref, r0, rows, n):
    zeros = jnp.zeros((SEQ_PAD, MIX), ref.dtype)
    top = zeros if r0 == 0 else ref[0, r0 - SEQ_PAD:r0, :]
    bot = zeros if r0 + rows == n else ref[0, r0 + rows:r0 + rows + SEQ_PAD, :]
    return jnp.concatenate([top, ref[0, r0:r0 + rows, :], bot], axis=0).astype(F32)


def _shift_rows(t, s):
    return pltpu.roll(t, s % t.shape[0], 0)


def _conv_step(cb_ref, cg_ref, cx_ref, cw_ref, conv_ref, *, n, fill):
    rows = min(n, CONV_ROWS)
    main = slice(SEQ_PAD, SEQ_PAD + rows)
    cw = cw_ref[0]
    for r0 in range(0, n, rows):
        fill()
        u = _padded_rows(cg_ref, r0, rows, n) * _padded_rows(cx_ref, r0, rows, n)
        conv = _shift_rows(u, 1) * cw[0:1] + u * cw[1:2] + _shift_rows(u, -1) * cw[2:3]
        conv_ref[0, r0:r0 + rows, :] = (
            cb_ref[0, r0:r0 + rows, :].astype(F32) * conv[main]).astype(BF16)


def _pool_step(pu_ref, pw_ref, ps_ref, norm_ref, pool_ref, *, n, fill):
    rows = min(n, CONV_ROWS)
    main = slice(SEQ_PAD, SEQ_PAD + rows)
    in_first_group = lax.broadcasted_iota(jnp.int32, (rows, LANES), 1) < POOL_GROUP
    for r0 in range(0, n, rows):
        fill()
        e_all = _padded_rows(pu_ref, r0, rows, n)
        halves = []
        for half in range(MIX // LANES):
            e = e_all[:, half * LANES:(half + 1) * LANES]
            c = e + _shift_rows(e, 1)
            sums = [c]
            for w in POOL_WINDOWS[1:2 * half + 2]:
                c = _shift_rows(c, w // 4) + _shift_rows(c, -(w // 4))
                sums.append(c)
            halves.append(jnp.where(in_first_group, sums[-2][main], sums[-1][main]))
        win = jnp.concatenate(halves, axis=1)
        mean = win * norm_ref[1, 0:1]
        if r0 == 0:
            mean = jnp.concatenate([win[:EDGE_ROWS] * norm_ref[0], mean[EDGE_ROWS:]], axis=0)
        if r0 + rows == n:
            mean = jnp.concatenate([mean[:-EDGE_ROWS], win[-EDGE_ROWS:] * norm_ref[2]], axis=0)
        pooled = (mean - e_all[main]).astype(BF16)
        pool_ref[0, r0:r0 + rows, :] = (_dot(pooled, pw_ref[0]) * ps_ref[0]).astype(BF16)


def _times_const(z, w):
    re, im = z

    def lin(a, ca, b, cb):
        acc = None
        for t, coef in ((a, ca), (b, cb)):
            if abs(coef) < 1e-12:
                continue
            unit = abs(abs(coef) - 1.0) < 1e-12
            term = t if unit else t * abs(coef)
            if acc is None:
                acc = term if coef > 0 else -term
            else:
                acc = acc + term if coef > 0 else acc - term
        return acc

    return lin(re, w.real, im, -w.imag), lin(re, w.imag, im, w.real)


def _dft_blocks(xs):
    count = len(xs)
    if count == 1:
        return xs
    even, odd = _dft_blocks(xs[0::2]), _dft_blocks(xs[1::2])
    out = [None] * count
    for k in range(count // 2):
        t = _times_const(odd[k], cmath.exp(-2j * math.pi * k / count))
        out[k] = (even[k][0] + t[0], even[k][1] + t[1])
        out[k + count // 2] = (even[k][0] - t[0], even[k][1] - t[1])
    return out


def _fourier_step(u_ref, cc_ref, ssn_ref, tw_ref, dft_ref, o_ref, spec_s, *, n, fill):
    m = DFT_BLOCK
    r_cnt = n // m
    scale = 1.0 / math.sqrt(n * FFT_GROUP_DIM)
    blocks = []
    for r in range(r_cnt):
        u = u_ref[0, r * m:(r + 1) * m, :]
        blocks.append((_dot(u, cc_ref[...]), _dot(u, ssn_ref[...])))
        fill()
    spectra = _dft_blocks(blocks)
    for k1 in range(r_cnt):
        fill()
        re, im = spectra[k1]
        if k1 > 0:
            tc = jnp.concatenate([tw_ref[0, k1]] * (MIX // LANES), axis=1)
            ts = jnp.concatenate([tw_ref[1, k1]] * (MIX // LANES), axis=1)
            re, im = re * tc + im * ts, im * tc - re * ts
        fill()
        rhs = jnp.concatenate([re.astype(BF16), im.astype(BF16)], axis=0)
        block = _dot(dft_ref[...], rhs) * scale
        fill()
        for half in range(MIX // LANES):
            spec_s[half, pl.ds(k1, m, stride=r_cnt), :] = block[:, half * LANES:(half + 1) * LANES]
    o_ref[0] = jnp.concatenate([spec_s[half] for half in range(MIX // LANES)],
                               axis=1).astype(BF16)


def _mixer_kernel(h_ref, wg_ref, a_ref, b_ref, c_ref, d_ref, rd_ref, s0_ref, avg_ref,
                  cw_ref, pw_ref, ps_ref, norm_ref, cc_ref, ssn_ref, tw_ref, dft_ref,
                  pg_ref, ret_ref, sout_ref, conv_ref, pool_ref, four_ref,
                  sf_s, sb_s, dm_s, dec_s, spec_s, *, n):
    step = pl.program_id(1)

    def gate_filler(j, row_chunk):
        rows = min(n, row_chunk)
        todo = [(r0, c0) for r0 in range(0, n, rows) for c0 in range(0, D_MODEL, GATE_COLS)]
        calls_per_piece = max(1, (n // CONV_ROWS) // len(todo))
        pieces = iter(todo)
        calls = iter(range(n))

        def emit():
            r0, c0 = next(pieces, (None, None))
            if r0 is not None:
                w = wg_ref[0, :, j * D_MODEL + c0:j * D_MODEL + c0 + GATE_COLS]
                gate = _sigmoid(_dot(h_ref[0, r0:r0 + rows, :], w))
                pg_ref[0, r0:r0 + rows, c0:c0 + GATE_COLS] = gate.astype(BF16)

        def fill():
            if next(calls) % calls_per_piece == 0:
                emit()

        def drain():
            for _ in todo:
                emit()

        return fill, drain

    @pl.when(step == 0)
    def _():
        fill, drain = gate_filler(0, GATE_ROWS_MXU_STEP)
        _retention_step(a_ref, b_ref, c_ref, d_ref, rd_ref, s0_ref, avg_ref, ret_ref, sout_ref,
                        sf_s, sb_s, dm_s, dec_s, n=n, fill=fill)
        drain()

    @pl.when(step == 1)
    def _():
        fill, drain = gate_filler(1, GATE_ROWS_VPU_STEP)
        _conv_step(a_ref, b_ref, c_ref, cw_ref, conv_ref, n=n, fill=fill)
        drain()

    @pl.when(step == 2)
    def _():
        fill, drain = gate_filler(2, GATE_ROWS_MXU_STEP)
        _fourier_step(a_ref, cc_ref, ssn_ref, tw_ref, dft_ref, four_ref, spec_s, n=n, fill=fill)
        drain()

    @pl.when(step == 3)
    def _():
        fill, drain = gate_filler(3, GATE_ROWS_VPU_STEP)
        _pool_step(d_ref, pw_ref, ps_ref, norm_ref, pool_ref, n=n, fill=fill)
        drain()


_SLOT_COLUMNS = ((0, 4, 7, 7), (1, 5, 5, 5), (2, 6, 6, 6), (3, 3, 3, 8))


def _mixer(h, pa, wg, rd_lane, s0, avg, conv_w, pool_bd, pool_scale, pool_norm,
           cc, ssn, tw, dft, layer):
    b, n, _ = pa.shape
    nc = n // RET_CHUNK

    def slot(cols):
        def index(i, j):
            col = jnp.where(j == 0, cols[0],
                            jnp.where(j == 1, cols[1], jnp.where(j == 2, cols[2], cols[3])))
            return (i, 0, col)
        return pl.BlockSpec((1, n, MIX), index)

    seq_out = pl.BlockSpec((1, n, MIX), lambda i, j: (i, 0, 0))
    state = pl.BlockSpec((1, 2, MIX, MIX), lambda i, j: (i, 0, 0, 0))
    return pl.pallas_call(
        functools.partial(_mixer_kernel, n=n),
        out_shape=(jax.ShapeDtypeStruct((b, n, PG_WIDTH), BF16),
                   jax.ShapeDtypeStruct((b, n, MIX), BF16),
                   jax.ShapeDtypeStruct((b, 2, MIX, MIX), F32),
                   jax.ShapeDtypeStruct((b, n, MIX), BF16),
                   jax.ShapeDtypeStruct((b, n, MIX), BF16),
                   jax.ShapeDtypeStruct((b, n, MIX), BF16)),
        grid=(b, 4),
        in_specs=[pl.BlockSpec((1, n, D_MODEL), lambda i, j: (i, 0, 0)),
                  _layer_spec(wg.shape, 0),
                  slot(_SLOT_COLUMNS[0]), slot(_SLOT_COLUMNS[1]),
                  slot(_SLOT_COLUMNS[2]), slot(_SLOT_COLUMNS[3]),
                  _layer_spec(rd_lane.shape, layer), state, _const_spec(avg.shape),
                  _layer_spec(conv_w.shape, layer), _layer_spec(pool_bd.shape, layer),
                  _layer_spec(pool_scale.shape, layer), _const_spec(pool_norm.shape),
                  _const_spec(cc.shape), _const_spec(ssn.shape), _const_spec(tw.shape),
                  _const_spec(dft.shape)],
        out_specs=(pl.BlockSpec((1, n, D_MODEL), lambda i, j: (i, 0, j)),
                   seq_out, state, seq_out, seq_out, seq_out),
        scratch_shapes=[pltpu.VMEM((nc, MIX, MIX), BF16), pltpu.VMEM((nc, MIX, MIX), BF16),
                        pltpu.VMEM((RET_HEADS, RET_CHUNK, RET_CHUNK), F32),
                        pltpu.VMEM((6, RET_CHUNK, MIX), F32),
                        pltpu.VMEM((MIX // LANES, n, LANES), F32)],
        compiler_params=_params("arbitrary", "arbitrary"),
        name="mixer",
    )(h, wg, pa, pa, pa, pa, rd_lane, s0, avg, conv_w, pool_bd, pool_scale, pool_norm,
      cc, ssn, tw, dft)


def _state_kernel(k_ref, v_ref, rd_ref, s0_ref, sout_ref, sf_s, sb_s, dm_s, dec_s, *, n):
    _retention_states(k_ref, v_ref, rd_ref, s0_ref, sout_ref, sf_s, sb_s, dm_s, dec_s, n=n)


def _final_states(pa, rd_lane, s0, layer):
    b, n, _ = pa.shape
    nc = n // RET_CHUNK
    state = pl.BlockSpec((1, 2, MIX, MIX), lambda i: (i, 0, 0, 0))
    return pl.pallas_call(
        functools.partial(_state_kernel, n=n),
        out_shape=jax.ShapeDtypeStruct((b, 2, MIX, MIX), F32),
        grid=(b,),
        in_specs=[pl.BlockSpec((1, n, MIX), lambda i: (i, 0, 1)),
                  pl.BlockSpec((1, n, MIX), lambda i: (i, 0, 2)),
                  _layer_spec(rd_lane.shape, layer), state],
        out_specs=state,
        scratch_shapes=[pltpu.VMEM((nc, MIX, MIX), BF16), pltpu.VMEM((nc, MIX, MIX), BF16),
                        pltpu.VMEM((RET_HEADS, RET_CHUNK, RET_CHUNK), F32),
                        pltpu.VMEM((6, RET_CHUNK, MIX), F32)],
        compiler_params=_params("arbitrary"),
        name="final_states",
    )(pa, pa, rd_lane, s0)


FFN_SPLITS = (0, 768, 1536, 2304, D_FF)
POST_ROWS = 256


def _post_kernel(b0_ref, b1_ref, b2_ref, b3_ref, pg_ref, x_ref, mod_ref, g_ref,
                 wb_ref, wo_ref, wup_ref, wdn_ref, *rest):
    if len(rest) == 1:
        o_ref, = rest
        cast_jobs = []
    else:
        o_ref = rest[5]
        cast_jobs = [functools.partial(_cast_slabs, *rest[:5], *rest[6:])]
    mod = mod_ref[0, 0]
    g = g_ref[0]

    def sub_tile(rows):
        merged = None
        for i, br in enumerate((b0_ref, b1_ref, b2_ref, b3_ref)):
            gate = pg_ref[0, rows, i * D_MODEL:(i + 1) * D_MODEL].astype(F32)
            t = gate * _dot(br[0, rows, :], wb_ref[0, i])
            merged = t if merged is None else merged + t
            if i % 2:
                yield "mix"
        mix = _dot(merged.astype(BF16), wo_ref[0])
        x1 = x_ref[0, rows, :] + _rms(mix, g[1:2] * mod[2:3])
        yield "mix"
        h = (_rms(x1, g[2:3] * (1.0 + mod[4:5])) + mod[3:4]).astype(BF16)
        yield "mix"
        ffn = None
        for lo, hi in zip(FFN_SPLITS[:-1], FFN_SPLITS[1:]):
            a = _dot(h, wup_ref[0, :, lo:hi])
            u = _dot(h, wup_ref[0, :, D_FF + lo:D_FF + hi])
            t = _dot((_silu(a) * u).astype(BF16), wdn_ref[0, lo:hi, :])
            ffn = t if ffn is None else ffn + t
            yield "ffn"
        o_ref[0, rows, :] = x1 + _rms(ffn, g[3:4] * mod[5:6])
        yield "done"

    tm = o_ref.shape[1]
    tiles = [sub_tile(pl.ds(r0, POST_ROWS)) for r0 in range(0, tm, POST_ROWS)]
    mix_pieces = ffn_chunks = len(FFN_SPLITS) - 1
    for _ in range(mix_pieces):
        next(tiles[0])
    for idx, cur in enumerate(tiles):
        for chunk in range(ffn_chunks):
            next(cur)
            if chunk == 0 and idx > 0:
                next(tiles[idx - 1])
            if idx + 1 < len(tiles):
                next(tiles[idx + 1])
            elif chunk >= 1 and cast_jobs:
                cast_jobs.pop()()
    next(tiles[-1])


def _post(branches, pg, x, mod, norm_g, weights, layer, mod_row, tm, cast_next=None):
    b, n, _ = x.shape
    tiles = n // tm
    br = pl.BlockSpec((1, tm, MIX), lambda i, j: (i, j, 0))
    in_specs = [br, br, br, br,
                pl.BlockSpec((1, tm, PG_WIDTH), lambda i, j: (i, j, 0)),
                pl.BlockSpec((1, tm, D_MODEL), lambda i, j: (i, j, 0)),
                _mod_spec(layer, mod_row),
                _layer_spec(norm_g.shape, layer)]
    in_specs += [_layer_spec(w.shape, 0) for w in weights]
    out_shape = [jax.ShapeDtypeStruct((b, n, D_MODEL), F32)]
    out_specs = [pl.BlockSpec((1, tm, D_MODEL), lambda i, j: (i, j, 0))]
    operands = [*branches, pg, x, mod, norm_g, *weights]
    if cast_next is not None:
        cast = _cast_plan(cast_next, layer + 1, b * tiles, lambda i, j: i * tiles + j)
        operands += cast.operands
        in_specs += cast.in_specs
        out_shape += cast.out_shape
        out_specs += cast.out_specs
    out = pl.pallas_call(
        _post_kernel,
        out_shape=tuple(out_shape),
        grid=(b, tiles),
        in_specs=in_specs,
        out_specs=tuple(out_specs),
        compiler_params=_params("arbitrary", "arbitrary"),
        name="merge_ffn",
    )(*operands)
    if cast_next is None:
        return out[0]
    return out[0], _layer_weights(out[1:])


class _CastPlan(NamedTuple):
    operands: list
    in_specs: list
    out_shape: list
    out_specs: list


def _cast_plan(stacked, layer, steps, step_of):
    w_in, w_branch, w_o, w_up, w_down = stacked
    sources = (w_in, w_branch.reshape(DEPTH, N_BRANCH * MIX, D_MODEL), w_o, w_up, w_down)
    plan = _CastPlan([], [], [], [])
    for src in sources:
        rows, width = src.shape[1:]
        share = 1 if (rows // steps) % 16 == 0 else 2
        height = rows * share // steps

        def src_index(*idx, share=share):
            return (layer, step_of(*idx) // share, 0)

        def dst_index(*idx, share=share):
            return (0, step_of(*idx) // share, 0)

        plan.operands.append(src)
        plan.in_specs.append(pl.BlockSpec((1, height, width), src_index))
        for out_width in ((PA_WIDTH, PG_WIDTH) if src is w_in else (width,)):
            plan.out_shape.append(jax.ShapeDtypeStruct((1, rows, out_width), BF16))
            plan.out_specs.append(pl.BlockSpec((1, height, out_width), dst_index))
    return plan


def _cast_slabs(win_f, wb_f, wo_f, wup_f, wdn_f, wa_o, wg_o, wb_o, wo_o, wup_o, wdn_o):
    wa_o[0] = win_f[0, :, :PA_WIDTH].astype(BF16)
    wg_o[0] = win_f[0, :, PA_WIDTH:].astype(BF16)
    for src, dst in ((wb_f, wb_o), (wo_f, wo_o), (wup_f, wup_o), (wdn_f, wdn_o)):
        dst[0] = src[0].astype(BF16)


def _layer_weights(cast_outputs):
    wa, wg, wb, wo, wup, wdn = cast_outputs
    return wa, wg, wb.reshape(1, N_BRANCH, MIX, D_MODEL), wo, wup, wdn


def _cast_weights(stacked, layer):
    steps = 32
    plan = _cast_plan(stacked, layer, steps, lambda s: s)
    out = pl.pallas_call(
        _cast_slabs,
        out_shape=tuple(plan.out_shape),
        grid=(steps,),
        in_specs=plan.in_specs,
        out_specs=tuple(plan.out_specs),
        compiler_params=_params("arbitrary"),
        name="cast_weights",
    )(*plan.operands)
    return _layer_weights(out)


def _rope_tables(n):
    rows = np.repeat(np.arange(n // GRID_W, dtype=np.float64), GRID_W)
    cols = np.tile(np.arange(GRID_W, dtype=np.float64), n // GRID_W)
    lane = np.arange(MIX)
    in_head = lane % RET_HEAD_DIM
    pos = np.where((in_head < RET_HEAD_DIM // 2)[None, :], rows[:, None], cols[:, None])
    quarter = RET_HEAD_DIM // 4
    freq = ROPE_BASE ** (-np.arange(quarter, dtype=np.float64) / quarter)
    ang = pos * freq[lane % quarter][None, :]
    sign = np.where((lane % (2 * quarter)) < quarter, -1.0, 1.0)[None, :]
    return np.cos(ang).astype(np.float32), (np.sin(ang) * sign).astype(np.float32)


def _dft_tables(n):
    m = DFT_BLOCK
    r_cnt = n // m
    k = np.arange(m, dtype=np.int64)
    ang = 2.0 * np.pi * ((k[:, None] * k[None, :]) % m) / m
    dft = np.concatenate([np.cos(ang), np.sin(ang)], axis=1).astype(np.float32)
    angt = 2.0 * np.pi * (np.arange(r_cnt)[:, None] * k[None, :]) / n
    tw = np.stack([np.cos(angt), np.sin(angt)])[..., None].repeat(LANES, axis=-1).astype(np.float32)
    ch = np.arange(FFT_GROUP_DIM, dtype=np.int64)
    angc = 2.0 * np.pi * ((ch[:, None] * ch[None, :]) % FFT_GROUP_DIM) / FFT_GROUP_DIM
    eye = np.eye(FFT_GROUPS)
    cc = np.kron(eye, np.cos(angc)).astype(np.float32)
    ssn = np.kron(eye, -np.sin(angc)).astype(np.float32)
    return dft, tw, cc, ssn


def _pool_norm(n):
    t = np.arange(n)
    cols = []
    for w in POOL_WINDOWS:
        cnt = np.clip(t - w // 2 + w, 0, n) - np.clip(t - w // 2, 0, n)
        inv = 1.0 / cnt
        assert np.all(cnt[EDGE_ROWS:n - EDGE_ROWS] == w)
        col = np.stack([inv[:EDGE_ROWS], np.full(EDGE_ROWS, 1.0 / w), inv[n - EDGE_ROWS:]])
        cols.append(np.repeat(col[:, :, None], POOL_GROUP, axis=2))
    return np.concatenate(cols, axis=2).astype(np.float32)


def _block_diag(w):
    nl, g, d, _ = w.shape
    eye = jnp.eye(g, dtype=w.dtype)
    return (eye[None, :, None, :, None] * w[:, :, :, None, :]).reshape(nl, g * d, g * d)


def kernel(x, c, ctx, c_ctx, w_mod, b_mod, norm_g, w_in, ret_decay, conv_w, pool_w, pool_scale,
           w_branch, w_o, ffn_w_up, ffn_w_down):
    b, n, _ = x.shape
    n_ctx = ctx.shape[1]

    cc_rows = jnp.concatenate(
        [c, c_ctx[None, :], jnp.zeros((MOD_ROWS - b - 1, D_MODEL), F32)], axis=0)
    mod = _modulation(cc_rows, w_mod, b_mod).reshape(DEPTH, MOD_ROWS, 6, D_MODEL)
    ctx_row = b

    cos_l, sin_l = (jnp.asarray(t) for t in _rope_tables(n))
    cos_c, sin_c = jnp.ones((n_ctx, MIX), F32), jnp.zeros((n_ctx, MIX), F32)

    def fourier_tables(length):
        dft, tw, cc, ssn = _dft_tables(length)
        return (jnp.asarray(cc).astype(BF16), jnp.asarray(ssn).astype(BF16), jnp.asarray(tw),
                jnp.asarray(dft).astype(BF16))

    tabs_l, tabs_c = fourier_tables(n), fourier_tables(n_ctx)
    norm_l, norm_c = jnp.asarray(_pool_norm(n)), jnp.asarray(_pool_norm(n_ctx))
    avg = jnp.asarray(np.kron(np.eye(RET_HEADS), np.full((RET_HEAD_DIM, RET_HEAD_DIM),
                                                         1.0 / RET_HEAD_DIM)), F32).astype(BF16)
    s_zero = jnp.zeros((b, 2, MIX, MIX), F32)

    w_stacked = (w_in, w_branch, w_o, ffn_w_up, ffn_w_down)
    w_layer = _cast_weights(w_stacked, 0)
    pool_bd = _block_diag(pool_w).astype(BF16)
    rd_lane = jnp.repeat(ret_decay.astype(F32), RET_HEAD_DIM, axis=2)
    ps = pool_scale.reshape(DEPTH, 1, MIX)

    for l in range(DEPTH):
        need_ctx = l < DEPTH - 1
        wa, wg = w_layer[:2]
        if need_ctx:
            h_c, pa_c = _in_proj(ctx, mod, norm_g, wa, cos_c, sin_c, l, ctx_row, tm=n_ctx)
            pg_c, ret_c, s_ctx, conv_c, pool_c, four_c = _mixer(
                h_c, pa_c, wg, rd_lane, s_zero, avg, conv_w, pool_bd, ps, norm_c, *tabs_c, l)
        else:
            _, qkv_c = _in_proj(ctx, mod, norm_g, wa, cos_c, sin_c, l, ctx_row, tm=n_ctx,
                                pa_width=3 * MIX)
            s_ctx = _final_states(qkv_c, rd_lane, s_zero, l)

        h_l, pa_l = _in_proj(x, mod, norm_g, wa, cos_l, sin_l, l, None, tm=1024)
        pg_l, ret_l, _, conv_l, pool_l, four_l = _mixer(
            h_l, pa_l, wg, rd_lane, s_ctx, avg, conv_w, pool_bd, ps, norm_l, *tabs_l, l)
        if need_ctx:
            ctx = _post((ret_c, conv_c, four_c, pool_c), pg_c, ctx, mod, norm_g, w_layer[2:],
                        l, ctx_row, tm=n_ctx)
            x, w_layer = _post((ret_l, conv_l, four_l, pool_l), pg_l, x, mod, norm_g, w_layer[2:],
                               l, None, tm=512, cast_next=w_stacked)
        else:
            x = _post((ret_l, conv_l, four_l, pool_l), pg_l, x, mod, norm_g, w_layer[2:],
                      l, None, tm=512)
    return x
```

```python
import cmath
import functools
import math
from typing import NamedTuple

import numpy as np
import jax
import jax.numpy as jnp
from jax import lax
from jax.experimental import pallas as pl
from jax.experimental.pallas import tpu as pltpu

D_MODEL = 1024
DEPTH = 4
GRID_W = 64
MIX = 256
N_BRANCH = 4
RET_HEADS = 4
RET_HEAD_DIM = MIX // RET_HEADS
POOL_WINDOWS = (2, 4, 8, 16)
POOL_GROUP = MIX // len(POOL_WINDOWS)
FFT_GROUPS = 4
FFT_GROUP_DIM = MIX // FFT_GROUPS
D_FF = 2816
PA_WIDTH = 9 * MIX
PG_WIDTH = N_BRANCH * D_MODEL
ROPE_BASE = 10000.0
EPS = 1e-6

LANES = 128
RET_CHUNK = 256
RET_GROUP = 4
DFT_BLOCK = 256
SEQ_PAD = 16
EDGE_ROWS = 8
GATE_ROWS_MXU_STEP = 512
GATE_ROWS_VPU_STEP = 256
GATE_COLS = 256
CONV_ROWS = 64
CTX_TILE = 1024
MOD_ROWS = 16
VMEM_LIMIT = 60 * 1024 * 1024

F32 = jnp.float32
BF16 = jnp.bfloat16


def _dot(a, b):
    return jnp.dot(a, b, preferred_element_type=F32)


def _rms(xf, g):
    ms = jnp.mean(xf * xf, axis=-1, keepdims=True)
    return xf * lax.rsqrt(ms + EPS) * g


def _sigmoid(x):
    return 0.5 * jnp.tanh(0.5 * x) + 0.5


def _silu(x):
    return x * _sigmoid(x)


def _const_spec(shape):
    zeros = (0,) * len(shape)
    return pl.BlockSpec(shape, lambda *_: zeros, pipeline_mode=pl.Buffered(1))


def _layer_spec(shape, layer):
    tail = (0,) * (len(shape) - 1)
    return pl.BlockSpec((1,) + tuple(shape[1:]), lambda *_: (layer,) + tail,
                        pipeline_mode=pl.Buffered(1))


def _mod_spec(row):
    return pl.BlockSpec((1, 1, 6, D_MODEL),
                        lambda i, j: (0, i if row is None else row, 0, 0))


def _params(*sem):
    return pltpu.CompilerParams(dimension_semantics=sem, vmem_limit_bytes=VMEM_LIMIT)


def _in_kernel(x_ref, mod_ref, g_ref, w_ref, cos_ref, sin_ref, h_ref, pa_ref):
    mod = mod_ref[0, 0]
    h = _rms(x_ref[0], g_ref[0, 0:1] * (1.0 + mod[1:2])) + mod[0:1]
    hb = h.astype(BF16)
    h_ref[0] = hb
    cos, sin = cos_ref[...], sin_ref[...]
    first_of_pair = (lax.broadcasted_iota(jnp.int32, cos.shape, 1) & 16) == 0

    def rope(t, scale):
        partner = jnp.where(first_of_pair, pltpu.roll(t, MIX - 16, 1), pltpu.roll(t, 16, 1))
        return (t * cos + partner * sin) * scale

    for j in range(0, pa_ref.shape[-1], 3 * MIX):
        res = _dot(hb, w_ref[0, :, j:j + 3 * MIX])
        if j == 0:
            pa_ref[0, :, 0:MIX] = rope(res[:, 0:MIX], RET_HEAD_DIM ** -0.5).astype(BF16)
            pa_ref[0, :, MIX:2 * MIX] = rope(res[:, MIX:2 * MIX], 1.0).astype(BF16)
            pa_ref[0, :, 2 * MIX:3 * MIX] = res[:, 2 * MIX:3 * MIX].astype(BF16)
        else:
            pa_ref[0, :, j:j + 3 * MIX] = res.astype(BF16)


def _in_proj(x, mod, norm_g, w_in, cos, sin, layer, mod_row, tm, pa_width=PA_WIDTH):
    b, n, _ = x.shape
    return pl.pallas_call(
        _in_kernel,
        out_shape=(jax.ShapeDtypeStruct((b, n, D_MODEL), BF16),
                   jax.ShapeDtypeStruct((b, n, pa_width), BF16)),
        grid=(b, n // tm),
        in_specs=[
            pl.BlockSpec((1, tm, D_MODEL), lambda i, j: (i, j, 0)),
            _mod_spec(mod_row),
            _layer_spec(norm_g.shape, layer),
            pl.BlockSpec((1, D_MODEL, pa_width), lambda i, j: (0, 0, 0),
                         pipeline_mode=pl.Buffered(1)),
            pl.BlockSpec((tm, MIX), lambda i, j: (j, 0)),
            pl.BlockSpec((tm, MIX), lambda i, j: (j, 0)),
        ],
        out_specs=(pl.BlockSpec((1, tm, D_MODEL), lambda i, j: (i, j, 0)),
                   pl.BlockSpec((1, tm, pa_width), lambda i, j: (i, j, 0))),
        compiler_params=_params("parallel", "parallel"),
        name="in_proj",
    )(x, mod, norm_g, w_in, cos, sin)


def _retention_states(k_ref, v_ref, rd_ref, s0_ref, sout_ref, sf_s, sb_s, dm_s, dec_s, *, n,
                      fill=lambda: None):
    c_len = RET_CHUNK
    nc = n // c_len

    @pl.when(pl.program_id(0) == 0)
    def _decay_tables():
        rd = rd_ref[0]
        lg = jnp.minimum(rd, 0.0) - jnp.log1p(jnp.exp(-jnp.abs(rd)))
        lgf, lgb = lg[0:1], lg[1:2]
        idx = lax.broadcasted_iota(jnp.int32, (c_len, MIX), 0).astype(F32)
        dec_s[0] = jnp.exp(lgf * (idx + 1.0))
        dec_s[1] = jnp.exp(lgf * (c_len - 1.0 - idx))
        dec_s[2] = jnp.exp(lgb * (c_len - idx))
        dec_s[3] = jnp.exp(lgb * idx)
        dec_s[4] = jnp.exp(lgf * float(c_len)) + 0.0 * idx
        dec_s[5] = jnp.exp(lgb * float(c_len)) + 0.0 * idx
        ii = lax.broadcasted_iota(jnp.int32, (c_len, c_len), 0)
        jj = lax.broadcasted_iota(jnp.int32, (c_len, c_len), 1)
        diff = (ii - jj).astype(F32)
        for h in range(RET_HEADS):
            lo = h * RET_HEAD_DIM
            rate = jnp.where(diff >= 0.0, lgf[:, lo:lo + 1], -lgb[:, lo:lo + 1])
            dm_s[h] = jnp.exp(rate * diff)

    row_head = lax.broadcasted_iota(jnp.int32, (MIX, MIX), 0) // RET_HEAD_DIM
    col_head = lax.broadcasted_iota(jnp.int32, (MIX, MIX), 1) // RET_HEAD_DIM
    block_diag = row_head == col_head
    tn_dims = (((0,), (0,)), ((), ()))

    def advance(state, c, key_decay, chunk_decay):
        rows = pl.ds(c * c_len, c_len)
        kd = (k_ref[0, rows, :].astype(F32) * key_decay).astype(BF16)
        upd = lax.dot_general(kd, v_ref[0, rows, :], tn_dims, preferred_element_type=F32)
        return state * chunk_decay + jnp.where(block_diag, upd, 0.0)

    fwd, bwd = s0_ref[0, 0], s0_ref[0, 1]
    for i in range(nc):
        fill()
        sf_s[i] = fwd.astype(BF16)
        sb_s[nc - 1 - i] = bwd.astype(BF16)
        fwd = advance(fwd, i, dec_s[1], dec_s[4])
        bwd = advance(bwd, nc - 1 - i, dec_s[3], dec_s[5])
    sout_ref[0, 0] = fwd
    sout_ref[0, 1] = bwd


def _retention_step(q_ref, k_ref, v_ref, g_ref, rd_ref, s0_ref, avg_ref, o_ref, sout_ref,
                    sf_s, sb_s, dm_s, dec_s, *, n, fill):
    c_len = RET_CHUNK
    nc = n // c_len
    head = lax.broadcasted_iota(jnp.int32, (c_len, MIX), 1) // RET_HEAD_DIM
    _retention_states(k_ref, v_ref, rd_ref, s0_ref, sout_ref, sf_s, sb_s, dm_s, dec_s, n=n,
                      fill=fill)

    nt_dims = (((1,), (1,)), ((), ()))
    avg = avg_ref[...]
    for c0 in range(0, nc, RET_GROUP):
        group = range(c0, min(c0 + RET_GROUP, nc))
        scores, cross = {}, {}
        for c in group:
            fill()
            rows = pl.ds(c * c_len, c_len)
            q, k = q_ref[0, rows, :], k_ref[0, rows, :]
            qf = q.astype(F32)
            cross[c] = (_dot((qf * dec_s[0]).astype(BF16), sf_s[c])
                        + _dot((qf * dec_s[2]).astype(BF16), sb_s[c]))
            fill()
            scores[c] = [lax.dot_general(jnp.where(head == h, q, jnp.zeros_like(q)), k, nt_dims,
                                         preferred_element_type=F32) for h in range(RET_HEADS)]
        ys = {}
        for c in group:
            v = v_ref[0, pl.ds(c * c_len, c_len), :]
            probs = [(scores[c][h] * dm_s[h]).astype(BF16) for h in range(RET_HEADS)]
            vals = [jnp.where(head == h, v, jnp.zeros_like(v)) for h in range(RET_HEADS)]
            ys[c] = cross[c] + _dot(jnp.concatenate(probs, axis=1), jnp.concatenate(vals, axis=0))
        for c in group:
            fill()
            rows = pl.ds(c * c_len, c_len)
            d = ys[c] - _dot(ys[c].astype(BF16), avg)
            yn = d * lax.rsqrt(_dot((d * d).astype(BF16), avg) + EPS)
            o_ref[0, rows, :] = (yn * _silu(g_ref[0, rows, :].astype(F32))).astype(BF16)


def _padded_rows(ref, r0, rows, n):
    zeros = jnp.zeros((SEQ_PAD, MIX), ref.dtype)
    top = zeros if r0 == 0 else ref[0, r0 - SEQ_PAD:r0, :]
    bot = zeros if r0 + rows == n else ref[0, r0 + rows:r0 + rows + SEQ_PAD, :]
    return jnp.concatenate([top, ref[0, r0:r0 + rows, :], bot], axis=0).astype(F32)


def _shift_rows(t, s):
    return pltpu.roll(t, s % t.shape[0], 0)


def _conv_step(cb_ref, cg_ref, cx_ref, cw_ref, conv_ref, *, n, fill):
    rows = min(n, CONV_ROWS)
    main = slice(SEQ_PAD, SEQ_PAD + rows)
    cw = cw_ref[0]
    for r0 in range(0, n, rows):
        fill()
        u = _padded_rows(cg_ref, r0, rows, n) * _padded_rows(cx_ref, r0, rows, n)
        conv = _shift_rows(u, 1) * cw[0:1] + u * cw[1:2] + _shift_rows(u, -1) * cw[2:3]
        conv_ref[0, r0:r0 + rows, :] = (
            cb_ref[0, r0:r0 + rows, :].astype(F32) * conv[main]).astype(BF16)


def _pool_step(pu_ref, pw_ref, ps_ref, norm_ref, pool_ref, *, n, fill):
    rows = min(n, CONV_ROWS)
    main = slice(SEQ_PAD, SEQ_PAD + rows)
    in_first_group = lax.broadcasted_iota(jnp.int32, (rows, LANES), 1) < POOL_GROUP
    for r0 in range(0, n, rows):
        fill()
        e_all = _padded_rows(pu_ref, r0, rows, n)
        halves = []
        for half in range(MIX // LANES):
            e = e_all[:, half * LANES:(half + 1) * LANES]
            c = e + _shift_rows(e, 1)
            sums = [c]
            for w in POOL_WINDOWS[1:2 * half + 2]:
                c = _shift_rows(c, w // 4) + _shift_rows(c, -(w // 4))
                sums.append(c)
            halves.append(jnp.where(in_first_group, sums[-2][main], sums[-1][main]))
        win = jnp.concatenate(halves, axis=1)
        mean = win * norm_ref[1, 0:1]
        if r0 == 0:
            mean = jnp.concatenate([win[:EDGE_ROWS] * norm_ref[0], mean[EDGE_ROWS:]], axis=0)
        if r0 + rows == n:
            mean = jnp.concatenate([mean[:-EDGE_ROWS], win[-EDGE_ROWS:] * norm_ref[2]], axis=0)
        pooled = (mean - e_all[main]).astype(BF16)
        pool_ref[0, r0:r0 + rows, :] = (_dot(pooled, pw_ref[0]) * ps_ref[0]).astype(BF16)


def _times_const(z, w):
    re, im = z

    def lin(a, ca, b, cb):
        acc = None
        for t, coef in ((a, ca), (b, cb)):
            if abs(coef) < 1e-12:
                continue
            unit = abs(abs(coef) - 1.0) < 1e-12
            term = t if unit else t * abs(coef)
            if acc is None:
                acc = term if coef > 0 else -term
            else:
                acc = acc + term if coef > 0 else acc - term
        return acc

    return lin(re, w.real, im, -w.imag), lin(re, w.imag, im, w.real)


def _dft_blocks(xs):
    count = len(xs)
    if count == 1:
        return xs
    even, odd = _dft_blocks(xs[0::2]), _dft_blocks(xs[1::2])
    out = [None] * count
    for k in range(count // 2):
        t = _times_const(odd[k], cmath.exp(-2j * math.pi * k / count))
        out[k] = (even[k][0] + t[0], even[k][1] + t[1])
        out[k + count // 2] = (even[k][0] - t[0], even[k][1] - t[1])
    return out


def _fourier_step(u_ref, cc_ref, ssn_ref, tw_ref, dft_ref, o_ref, spec_s, *, n, fill):
    m = DFT_BLOCK
    r_cnt = n // m
    scale = 1.0 / math.sqrt(n * FFT_GROUP_DIM)
    blocks = []
    for r in range(r_cnt):
        u = u_ref[0, r * m:(r + 1) * m, :]
        blocks.append((_dot(u, cc_ref[...]), _dot(u, ssn_ref[...])))
        fill()
    spectra = _dft_blocks(blocks)
    for k1 in range(r_cnt):
        fill()
        re, im = spectra[k1]
        if k1 > 0:
            tc = jnp.concatenate([tw_ref[0, k1]] * (MIX // LANES), axis=1)
            ts = jnp.concatenate([tw_ref[1, k1]] * (MIX // LANES), axis=1)
            re, im = re * tc + im * ts, im * tc - re * ts
        fill()
        rhs = jnp.concatenate([re.astype(BF16), im.astype(BF16)], axis=0)
        block = _dot(dft_ref[...], rhs) * scale
        fill()
        for half in range(MIX // LANES):
            spec_s[half, pl.ds(k1, m, stride=r_cnt), :] = block[:, half * LANES:(half + 1) * LANES]
    o_ref[0] = jnp.concatenate([spec_s[half] for half in range(MIX // LANES)],
                               axis=1).astype(BF16)


def _mixer_kernel(h_ref, wg_ref, a_ref, b_ref, c_ref, d_ref, rd_ref, s0_ref, avg_ref,
                  cw_ref, pw_ref, ps_ref, norm_ref, cc_ref, ssn_ref, tw_ref, dft_ref,
                  pg_ref, ret_ref, sout_ref, conv_ref, pool_ref, four_ref,
                  sf_s, sb_s, dm_s, dec_s, spec_s, *, n):
    step = pl.program_id(1)

    def gate_filler(j, row_chunk):
        rows = min(n, row_chunk)
        todo = [(r0, c0) for r0 in range(0, n, rows) for c0 in range(0, D_MODEL, GATE_COLS)]
        calls_per_piece = max(1, (n // CONV_ROWS) // len(todo))
        pieces = iter(todo)
        calls = iter(range(n))

        def emit():
            r0, c0 = next(pieces, (None, None))
            if r0 is not None:
                w = wg_ref[0, :, j * D_MODEL + c0:j * D_MODEL + c0 + GATE_COLS]
                gate = _sigmoid(_dot(h_ref[0, r0:r0 + rows, :], w))
                pg_ref[0, r0:r0 + rows, c0:c0 + GATE_COLS] = gate.astype(BF16)

        def fill():
            if next(calls) % calls_per_piece == 0:
                emit()

        def drain():
            for _ in todo:
                emit()

        return fill, drain

    @pl.when(step == 0)
    def _():
        fill, drain = gate_filler(0, GATE_ROWS_MXU_STEP)
        _retention_step(a_ref, b_ref, c_ref, d_ref, rd_ref, s0_ref, avg_ref, ret_ref, sout_ref,
                        sf_s, sb_s, dm_s, dec_s, n=n, fill=fill)
        drain()

    @pl.when(step == 1)
    def _():
        fill, drain = gate_filler(1, GATE_ROWS_VPU_STEP)
        _conv_step(a_ref, b_ref, c_ref, cw_ref, conv_ref, n=n, fill=fill)
        drain()

    @pl.when(step == 2)
    def _():
        fill, drain = gate_filler(2, GATE_ROWS_MXU_STEP)
        _fourier_step(a_ref, cc_ref, ssn_ref, tw_ref, dft_ref, four_ref, spec_s, n=n, fill=fill)
        drain()

    @pl.when(step == 3)
    def _():
        fill, drain = gate_filler(3, GATE_ROWS_VPU_STEP)
        _pool_step(d_ref, pw_ref, ps_ref, norm_ref, pool_ref, n=n, fill=fill)
        drain()


_SLOT_COLUMNS = ((0, 4, 7, 7), (1, 5, 5, 5), (2, 6, 6, 6), (3, 3, 3, 8))


def _mixer(h, pa, wg, rd_lane, s0, avg, conv_w, pool_bd, pool_scale, pool_norm,
           cc, ssn, tw, dft, layer):
    b, n, _ = pa.shape
    nc = n // RET_CHUNK

    def slot(cols):
        def index(i, j):
            col = jnp.where(j == 0, cols[0],
                            jnp.where(j == 1, cols[1], jnp.where(j == 2, cols[2], cols[3])))
            return (i, 0, col)
        return pl.BlockSpec((1, n, MIX), index)

    seq_out = pl.BlockSpec((1, n, MIX), lambda i, j: (i, 0, 0))
    state = pl.BlockSpec((1, 2, MIX, MIX), lambda i, j: (i, 0, 0, 0))
    return pl.pallas_call(
        functools.partial(_mixer_kernel, n=n),
        out_shape=(jax.ShapeDtypeStruct((b, n, PG_WIDTH), BF16),
                   jax.ShapeDtypeStruct((b, n, MIX), BF16),
                   jax.ShapeDtypeStruct((b, 2, MIX, MIX), F32),
                   jax.ShapeDtypeStruct((b, n, MIX), BF16),
                   jax.ShapeDtypeStruct((b, n, MIX), BF16),
                   jax.ShapeDtypeStruct((b, n, MIX), BF16)),
        grid=(b, 4),
        in_specs=[pl.BlockSpec((1, n, D_MODEL), lambda i, j: (i, 0, 0)),
                  _layer_spec(wg.shape, 0),
                  slot(_SLOT_COLUMNS[0]), slot(_SLOT_COLUMNS[1]),
                  slot(_SLOT_COLUMNS[2]), slot(_SLOT_COLUMNS[3]),
                  _layer_spec(rd_lane.shape, layer), state, _const_spec(avg.shape),
                  _layer_spec(conv_w.shape, layer), _layer_spec(pool_bd.shape, layer),
                  _layer_spec(pool_scale.shape, layer), _const_spec(pool_norm.shape),
                  _const_spec(cc.shape), _const_spec(ssn.shape), _const_spec(tw.shape),
                  _const_spec(dft.shape)],
        out_specs=(pl.BlockSpec((1, n, D_MODEL), lambda i, j: (i, 0, j)),
                   seq_out, state, seq_out, seq_out, seq_out),
        scratch_shapes=[pltpu.VMEM((nc, MIX, MIX), BF16), pltpu.VMEM((nc, MIX, MIX), BF16),
                        pltpu.VMEM((RET_HEADS, RET_CHUNK, RET_CHUNK), F32),
                        pltpu.VMEM((6, RET_CHUNK, MIX), F32),
                        pltpu.VMEM((MIX // LANES, n, LANES), F32)],
        compiler_params=_params("arbitrary", "arbitrary"),
        name="mixer",
    )(h, wg, pa, pa, pa, pa, rd_lane, s0, avg, conv_w, pool_bd, pool_scale, pool_norm,
      cc, ssn, tw, dft)


def _state_kernel(k_ref, v_ref, rd_ref, s0_ref, sout_ref, sf_s, sb_s, dm_s, dec_s, *, n):
    _retention_states(k_ref, v_ref, rd_ref, s0_ref, sout_ref, sf_s, sb_s, dm_s, dec_s, n=n)


def _final_states(pa, rd_lane, s0, layer):
    b, n, _ = pa.shape
    nc = n // RET_CHUNK
    state = pl.BlockSpec((1, 2, MIX, MIX), lambda i: (i, 0, 0, 0))
    return pl.pallas_call(
        functools.partial(_state_kernel, n=n),
        out_shape=jax.ShapeDtypeStruct((b, 2, MIX, MIX), F32),
        grid=(b,),
        in_specs=[pl.BlockSpec((1, n, MIX), lambda i: (i, 0, 1)),
                  pl.BlockSpec((1, n, MIX), lambda i: (i, 0, 2)),
                  _layer_spec(rd_lane.shape, layer), state],
        out_specs=state,
        scratch_shapes=[pltpu.VMEM((nc, MIX, MIX), BF16), pltpu.VMEM((nc, MIX, MIX), BF16),
                        pltpu.VMEM((RET_HEADS, RET_CHUNK, RET_CHUNK), F32),
                        pltpu.VMEM((6, RET_CHUNK, MIX), F32)],
        compiler_params=_params("arbitrary"),
        name="final_states",
    )(pa, pa, rd_lane, s0)


FFN_SPLITS = (0, 768, 1536, 2304, D_FF)
POST_ROWS = 256


def _post_kernel(b0_ref, b1_ref, b2_ref, b3_ref, pg_ref, x_ref, mod_ref, g_ref,
                 wb_ref, wo_ref, wup_ref, wdn_ref, *rest):
    if len(rest) == 1:
        o_ref, = rest
        cast_jobs = []
    else:
        o_ref = rest[PREP_INPUTS]
        prep_refs = rest[:PREP_INPUTS] + rest[PREP_INPUTS + 1:]
        _prep_init(prep_refs[PREP_INPUTS - 1], prep_refs[-1],
                   (pl.program_id(0) == 0) & (pl.program_id(1) == 0))
        cast_jobs = [functools.partial(_prep_slabs, *prep_refs)]
    mod = mod_ref[0, 0]
    g = g_ref[0]

    def sub_tile(rows):
        merged = None
        for i, br in enumerate((b0_ref, b1_ref, b2_ref, b3_ref)):
            gate = pg_ref[0, rows, i * D_MODEL:(i + 1) * D_MODEL].astype(F32)
            t = gate * _dot(br[0, rows, :], wb_ref[0, i])
            merged = t if merged is None else merged + t
            if i % 2:
                yield "mix"
        mix = _dot(merged.astype(BF16), wo_ref[0])
        x1 = x_ref[0, rows, :] + _rms(mix, g[1:2] * mod[2:3])
        yield "mix"
        h = (_rms(x1, g[2:3] * (1.0 + mod[4:5])) + mod[3:4]).astype(BF16)
        yield "mix"
        ffn = None
        for lo, hi in zip(FFN_SPLITS[:-1], FFN_SPLITS[1:]):
            a = _dot(h, wup_ref[0, :, lo:hi])
            u = _dot(h, wup_ref[0, :, D_FF + lo:D_FF + hi])
            t = _dot((_silu(a) * u).astype(BF16), wdn_ref[0, lo:hi, :])
            ffn = t if ffn is None else ffn + t
            yield "ffn"
        o_ref[0, rows, :] = x1 + _rms(ffn, g[3:4] * mod[5:6])
        yield "done"

    tm = o_ref.shape[1]
    tiles = [sub_tile(pl.ds(r0, POST_ROWS)) for r0 in range(0, tm, POST_ROWS)]
    mix_pieces = ffn_chunks = len(FFN_SPLITS) - 1
    for _ in range(mix_pieces):
        next(tiles[0])
    for idx, cur in enumerate(tiles):
        for chunk in range(ffn_chunks):
            next(cur)
            if chunk == 0 and idx > 0:
                next(tiles[idx - 1])
            if idx + 1 < len(tiles):
                next(tiles[idx + 1])
            elif chunk >= 1 and cast_jobs:
                cast_jobs.pop()()
    next(tiles[-1])


def _post(branches, pg, x, mod, norm_g, weights, layer, mod_row, tm, cast_next=None):
    b, n, _ = x.shape
    tiles = n // tm
    br = pl.BlockSpec((1, tm, MIX), lambda i, j: (i, j, 0))
    in_specs = [br, br, br, br,
                pl.BlockSpec((1, tm, PG_WIDTH), lambda i, j: (i, j, 0)),
                pl.BlockSpec((1, tm, D_MODEL), lambda i, j: (i, j, 0)),
                _mod_spec(mod_row),
                _layer_spec(norm_g.shape, layer)]
    in_specs += [_layer_spec(w.shape, 0) for w in weights]
    out_shape = [jax.ShapeDtypeStruct((b, n, D_MODEL), F32)]
    out_specs = [pl.BlockSpec((1, tm, D_MODEL), lambda i, j: (i, j, 0))]
    operands = [*branches, pg, x, mod, norm_g, *weights]
    if cast_next is not None:
        cast = _cast_plan(cast_next, layer + 1, b * tiles, lambda i, j: i * tiles + j)
        operands += cast.operands
        in_specs += cast.in_specs
        out_shape += cast.out_shape
        out_specs += cast.out_specs
    out = pl.pallas_call(
        _post_kernel,
        out_shape=tuple(out_shape),
        grid=(b, tiles),
        in_specs=in_specs,
        out_specs=tuple(out_specs),
        compiler_params=_params("arbitrary", "arbitrary"),
        name="merge_ffn",
    )(*operands)
    if cast_next is None:
        return out[0]
    return (out[0],) + _layer_params(out[1:])


class _CastPlan(NamedTuple):
    operands: list
    in_specs: list
    out_shape: list
    out_specs: list


def _cast_plan(stacked, layer, steps, step_of):
    w_in, w_branch, w_o, w_up, w_down, c_t, w_mod, b_mod = stacked
    sources = (w_in, w_branch.reshape(DEPTH, N_BRANCH * MIX, D_MODEL), w_o, w_up, w_down)
    plan = _CastPlan([], [], [], [])
    for src in sources:
        rows, width = src.shape[1:]
        share = 1 if (rows // steps) % 16 == 0 else 2
        height = rows * share // steps

        def src_index(*idx, share=share):
            return (layer, step_of(*idx) // share, 0)

        def dst_index(*idx, share=share):
            return (0, step_of(*idx) // share, 0)

        plan.operands.append(src)
        plan.in_specs.append(pl.BlockSpec((1, height, width), src_index))
        for out_width in ((PA_WIDTH, PG_WIDTH) if src is w_in else (width,)):
            plan.out_shape.append(jax.ShapeDtypeStruct((1, rows, out_width), BF16))
            plan.out_specs.append(pl.BlockSpec((1, height, out_width), dst_index))

    slab = D_MODEL // steps
    n_out = w_mod.shape[-1]
    plan.operands.extend([c_t, w_mod, b_mod.reshape(DEPTH, 1, n_out)])
    plan.in_specs.extend([
        pl.BlockSpec((slab, MOD_ROWS), lambda *idx: (step_of(*idx), 0)),
        pl.BlockSpec((1, slab, n_out), lambda *idx: (layer, step_of(*idx), 0)),
        pl.BlockSpec((1, 1, n_out), lambda *idx: (layer, 0, 0))])
    plan.out_shape.append(jax.ShapeDtypeStruct((1, MOD_ROWS, n_out), F32))
    plan.out_specs.append(pl.BlockSpec((1, MOD_ROWS, n_out), lambda *idx: (0, 0, 0)))
    return plan


PREP_INPUTS = 8


def _prep_init(bm_ref, mod_o, first):
    @pl.when(first)
    def _():
        mod_o[0] = jnp.broadcast_to(bm_ref[0], mod_o.shape[1:])


def _prep_slabs(win_f, wb_f, wo_f, wup_f, wdn_f, ct_ref, wm_ref, bm_ref,
                wa_o, wg_o, wb_o, wo_o, wup_o, wdn_o, mod_o):
    wa_o[0] = win_f[0, :, :PA_WIDTH].astype(BF16)
    wg_o[0] = win_f[0, :, PA_WIDTH:].astype(BF16)
    for src, dst in ((wb_f, wb_o), (wo_f, wo_o), (wup_f, wup_o), (wdn_f, wdn_o)):
        dst[0] = src[0].astype(BF16)
    a = _silu(ct_ref[...]).astype(BF16)
    mod_o[0] += lax.dot_general(a, wm_ref[0].astype(BF16), (((0,), (0,)), ((), ())),
                                preferred_element_type=F32)


def _layer_params(prep_outputs):
    wa, wg, wb, wo, wup, wdn, mod = prep_outputs
    return ((wa, wg, wb.reshape(1, N_BRANCH, MIX, D_MODEL), wo, wup, wdn),
            mod.reshape(1, MOD_ROWS, 6, D_MODEL))


def _prep_kernel(*refs):
    _prep_init(refs[PREP_INPUTS - 1], refs[-1], pl.program_id(0) == 0)
    _prep_slabs(*refs)


def _prep_layer(stacked, layer):
    steps = 32
    plan = _cast_plan(stacked, layer, steps, lambda s: s)
    out = pl.pallas_call(
        _prep_kernel,
        out_shape=tuple(plan.out_shape),
        grid=(steps,),
        in_specs=plan.in_specs,
        out_specs=tuple(plan.out_specs),
        compiler_params=_params("arbitrary"),
        name="prep_layer",
    )(*plan.operands)
    return _layer_params(out)


def _rope_tables(n):
    rows = np.repeat(np.arange(n // GRID_W, dtype=np.float64), GRID_W)
    cols = np.tile(np.arange(GRID_W, dtype=np.float64), n // GRID_W)
    lane = np.arange(MIX)
    in_head = lane % RET_HEAD_DIM
    pos = np.where((in_head < RET_HEAD_DIM // 2)[None, :], rows[:, None], cols[:, None])
    quarter = RET_HEAD_DIM // 4
    freq = ROPE_BASE ** (-np.arange(quarter, dtype=np.float64) / quarter)
    ang = pos * freq[lane % quarter][None, :]
    sign = np.where((lane % (2 * quarter)) < quarter, -1.0, 1.0)[None, :]
    return np.cos(ang).astype(np.float32), (np.sin(ang) * sign).astype(np.float32)


def _dft_tables(n):
    m = DFT_BLOCK
    r_cnt = n // m
    k = np.arange(m, dtype=np.int64)
    ang = 2.0 * np.pi * ((k[:, None] * k[None, :]) % m) / m
    dft = np.concatenate([np.cos(ang), np.sin(ang)], axis=1).astype(np.float32)
    angt = 2.0 * np.pi * (np.arange(r_cnt)[:, None] * k[None, :]) / n
    tw = np.stack([np.cos(angt), np.sin(angt)])[..., None].repeat(LANES, axis=-1).astype(np.float32)
    ch = np.arange(FFT_GROUP_DIM, dtype=np.int64)
    angc = 2.0 * np.pi * ((ch[:, None] * ch[None, :]) % FFT_GROUP_DIM) / FFT_GROUP_DIM
    eye = np.eye(FFT_GROUPS)
    cc = np.kron(eye, np.cos(angc)).astype(np.float32)
    ssn = np.kron(eye, -np.sin(angc)).astype(np.float32)
    return dft, tw, cc, ssn


def _pool_norm(n):
    t = np.arange(n)
    cols = []
    for w in POOL_WINDOWS:
        cnt = np.clip(t - w // 2 + w, 0, n) - np.clip(t - w // 2, 0, n)
        inv = 1.0 / cnt
        assert np.all(cnt[EDGE_ROWS:n - EDGE_ROWS] == w)
        col = np.stack([inv[:EDGE_ROWS], np.full(EDGE_ROWS, 1.0 / w), inv[n - EDGE_ROWS:]])
        cols.append(np.repeat(col[:, :, None], POOL_GROUP, axis=2))
    return np.concatenate(cols, axis=2).astype(np.float32)


def _block_diag(w):
    nl, g, d, _ = w.shape
    eye = jnp.eye(g, dtype=w.dtype)
    return (eye[None, :, None, :, None] * w[:, :, :, None, :]).reshape(nl, g * d, g * d)


def kernel(x, c, ctx, c_ctx, w_mod, b_mod, norm_g, w_in, ret_decay, conv_w, pool_w, pool_scale,
           w_branch, w_o, ffn_w_up, ffn_w_down):
    b, n, _ = x.shape
    n_ctx = ctx.shape[1]

    c_t = jnp.concatenate(
        [c, c_ctx[None, :], jnp.zeros((MOD_ROWS - b - 1, D_MODEL), F32)], axis=0).T
    ctx_row = b

    cos_l, sin_l = (jnp.asarray(t) for t in _rope_tables(n))
    cos_c, sin_c = jnp.ones((b * n_ctx, MIX), F32), jnp.zeros((b * n_ctx, MIX), F32)
    ctx = ctx.reshape(1, b * n_ctx, D_MODEL)

    def per_sample(t):
        return t.reshape(b, n_ctx, t.shape[-1])

    def flat(t):
        return t.reshape(1, b * n_ctx, t.shape[-1]) if t.shape[1] == n_ctx else t

    def fourier_tables(length):
        dft, tw, cc, ssn = _dft_tables(length)
        return (jnp.asarray(cc).astype(BF16), jnp.asarray(ssn).astype(BF16), jnp.asarray(tw),
                jnp.asarray(dft).astype(BF16))

    tabs_l, tabs_c = fourier_tables(n), fourier_tables(n_ctx)
    norm_l, norm_c = jnp.asarray(_pool_norm(n)), jnp.asarray(_pool_norm(n_ctx))
    avg = jnp.asarray(np.kron(np.eye(RET_HEADS), np.full((RET_HEAD_DIM, RET_HEAD_DIM),
                                                         1.0 / RET_HEAD_DIM)), F32).astype(BF16)
    s_zero = jnp.zeros((b, 2, MIX, MIX), F32)

    w_stacked = (w_in, w_branch, w_o, ffn_w_up, ffn_w_down, c_t, w_mod, b_mod)
    w_layer, mod = _prep_layer(w_stacked, 0)
    pool_bd = _block_diag(pool_w).astype(BF16)
    rd_lane = jnp.repeat(ret_decay.astype(F32), RET_HEAD_DIM, axis=2)
    ps = pool_scale.reshape(DEPTH, 1, MIX)

    for l in range(DEPTH):
        need_ctx = l < DEPTH - 1
        wa, wg = w_layer[:2]
        if need_ctx:
            h_c, pa_c = _in_proj(ctx, mod, norm_g, wa, cos_c, sin_c, l, ctx_row, tm=CTX_TILE)
            mixed_c = _mixer(per_sample(h_c), per_sample(pa_c), wg, rd_lane, s_zero, avg, conv_w,
                             pool_bd, ps, norm_c, *tabs_c, l)
            s_ctx = mixed_c[2]
            pg_c, ret_c, _, conv_c, pool_c, four_c = (flat(t) for t in mixed_c)
        else:
            _, qkv_c = _in_proj(ctx, mod, norm_g, wa, cos_c, sin_c, l, ctx_row, tm=CTX_TILE,
                                pa_width=3 * MIX)
            s_ctx = _final_states(per_sample(qkv_c), rd_lane, s_zero, l)

        h_l, pa_l = _in_proj(x, mod, norm_g, wa, cos_l, sin_l, l, None, tm=1024)
        pg_l, ret_l, _, conv_l, pool_l, four_l = _mixer(
            h_l, pa_l, wg, rd_lane, s_ctx, avg, conv_w, pool_bd, ps, norm_l, *tabs_l, l)
        if need_ctx:
            ctx = _post((ret_c, conv_c, four_c, pool_c), pg_c, ctx, mod, norm_g, w_layer[2:],
                        l, ctx_row, tm=512)
            x, w_layer, mod = _post((ret_l, conv_l, four_l, pool_l), pg_l, x, mod, norm_g,
                                    w_layer[2:], l, None, tm=512, cast_next=w_stacked)
        else:
            x = _post((ret_l, conv_l, four_l, pool_l), pg_l, x, mod, norm_g, w_layer[2:],
                      l, None, tm=512)
    return x
```

```python
import cmath
import functools
import math
from typing import NamedTuple

import numpy as np
import jax
import jax.numpy as jnp
from jax import lax
from jax.experimental import pallas as pl
from jax.experimental.pallas import tpu as pltpu

D_MODEL = 1024
DEPTH = 4
GRID_W = 64
MIX = 256
N_BRANCH = 4
RET_HEADS = 4
RET_HEAD_DIM = MIX // RET_HEADS
POOL_WINDOWS = (2, 4, 8, 16)
POOL_GROUP = MIX // len(POOL_WINDOWS)
FFT_GROUPS = 4
FFT_GROUP_DIM = MIX // FFT_GROUPS
D_FF = 2816
PA_WIDTH = 9 * MIX
PG_WIDTH = N_BRANCH * D_MODEL
ROPE_BASE = 10000.0
EPS = 1e-6

LANES = 128
RET_CHUNK = 256
RET_GROUP = 4
DFT_BLOCK = 256
SEQ_PAD = 16
EDGE_ROWS = 8
GATE_ROWS_MXU_STEP = 512
GATE_ROWS_VPU_STEP = 256
GATE_COLS = 256
CONV_ROWS = 64
CTX_TILE = 256
IN_ROWS = 256
MOD_ROWS = 16
VMEM_LIMIT = 60 * 1024 * 1024

F32 = jnp.float32
BF16 = jnp.bfloat16


def _dot(a, b):
    return jnp.dot(a, b, preferred_element_type=F32)


def _rms(xf, g):
    ms = jnp.mean(xf * xf, axis=-1, keepdims=True)
    return xf * lax.rsqrt(ms + EPS) * g


def _sigmoid(x):
    return 0.5 * jnp.tanh(0.5 * x) + 0.5


def _silu(x):
    return x * _sigmoid(x)


def _const_spec(shape):
    zeros = (0,) * len(shape)
    return pl.BlockSpec(shape, lambda *_: zeros, pipeline_mode=pl.Buffered(1))


def _layer_spec(shape, layer):
    tail = (0,) * (len(shape) - 1)
    return pl.BlockSpec((1,) + tuple(shape[1:]), lambda *_: (layer,) + tail,
                        pipeline_mode=pl.Buffered(1))


def _mod_spec(row):
    return pl.BlockSpec((1, 1, 6, D_MODEL),
                        lambda i, j: (0, i if row is None else row, 0, 0))


def _params(*sem):
    return pltpu.CompilerParams(dimension_semantics=sem, vmem_limit_bytes=VMEM_LIMIT)


def _in_kernel(x_ref, mod_ref, g_ref, w_ref, cos_ref, sin_ref, h_ref, pa_ref):
    mod = mod_ref[0, 0]
    gain = g_ref[0, 0:1] * (1.0 + mod[1:2])
    tm = x_ref.shape[1]
    sub = min(tm, IN_ROWS)
    first_of_pair = (lax.broadcasted_iota(jnp.int32, (sub, MIX), 1) & 16) == 0

    def sub_tile(rows):
        hb = (_rms(x_ref[0, rows, :], gain) + mod[0:1]).astype(BF16)
        h_ref[0, rows, :] = hb
        yield
        cos, sin = cos_ref[rows, :], sin_ref[rows, :]

        def rope(t, scale):
            partner = jnp.where(first_of_pair, pltpu.roll(t, MIX - 16, 1), pltpu.roll(t, 16, 1))
            return (t * cos + partner * sin) * scale

        for j in range(0, pa_ref.shape[-1], 3 * MIX):
            res = _dot(hb, w_ref[0, :, j:j + 3 * MIX])
            if j == 0:
                pa_ref[0, rows, 0:MIX] = rope(res[:, 0:MIX], RET_HEAD_DIM ** -0.5).astype(BF16)
                pa_ref[0, rows, MIX:2 * MIX] = rope(res[:, MIX:2 * MIX], 1.0).astype(BF16)
                pa_ref[0, rows, 2 * MIX:3 * MIX] = res[:, 2 * MIX:3 * MIX].astype(BF16)
            else:
                pa_ref[0, rows, j:j + 3 * MIX] = res.astype(BF16)
            yield

    tiles = [sub_tile(pl.ds(r0, sub)) for r0 in range(0, tm, sub)]
    next(tiles[0])
    for idx, cur in enumerate(tiles):
        next(cur)
        if idx + 1 < len(tiles):
            next(tiles[idx + 1])
        for _ in cur:
            pass


def _in_proj(x, mod, norm_g, w_in, cos, sin, layer, mod_row, tm, pa_width=PA_WIDTH):
    b, n, _ = x.shape
    return pl.pallas_call(
        _in_kernel,
        out_shape=(jax.ShapeDtypeStruct((b, n, D_MODEL), BF16),
                   jax.ShapeDtypeStruct((b, n, pa_width), BF16)),
        grid=(b, n // tm),
        in_specs=[
            pl.BlockSpec((1, tm, D_MODEL), lambda i, j: (i, j, 0)),
            _mod_spec(mod_row),
            _layer_spec(norm_g.shape, layer),
            pl.BlockSpec((1, D_MODEL, pa_width), lambda i, j: (0, 0, 0),
                         pipeline_mode=pl.Buffered(1)),
            pl.BlockSpec((tm, MIX), lambda i, j: (j, 0)),
            pl.BlockSpec((tm, MIX), lambda i, j: (j, 0)),
        ],
        out_specs=(pl.BlockSpec((1, tm, D_MODEL), lambda i, j: (i, j, 0)),
                   pl.BlockSpec((1, tm, pa_width), lambda i, j: (i, j, 0))),
        compiler_params=_params("parallel", "parallel"),
        name="in_proj",
    )(x, mod, norm_g, w_in, cos, sin)


def _retention_states(k_ref, v_ref, rd_ref, s0_ref, sout_ref, sf_s, sb_s, dm_s, dec_s, *, n,
                      fill=lambda: None):
    c_len = RET_CHUNK
    nc = n // c_len

    @pl.when(pl.program_id(0) == 0)
    def _decay_tables():
        rd = rd_ref[0]
        lg = jnp.minimum(rd, 0.0) - jnp.log1p(jnp.exp(-jnp.abs(rd)))
        lgf, lgb = lg[0:1], lg[1:2]
        idx = lax.broadcasted_iota(jnp.int32, (c_len, MIX), 0).astype(F32)
        dec_s[0] = jnp.exp(lgf * (idx + 1.0))
        dec_s[1] = jnp.exp(lgf * (c_len - 1.0 - idx))
        dec_s[2] = jnp.exp(lgb * (c_len - idx))
        dec_s[3] = jnp.exp(lgb * idx)
        dec_s[4] = jnp.exp(lgf * float(c_len)) + 0.0 * idx
        dec_s[5] = jnp.exp(lgb * float(c_len)) + 0.0 * idx
        ii = lax.broadcasted_iota(jnp.int32, (c_len, c_len), 0)
        jj = lax.broadcasted_iota(jnp.int32, (c_len, c_len), 1)
        diff = (ii - jj).astype(F32)
        for h in range(RET_HEADS):
            lo = h * RET_HEAD_DIM
            rate = jnp.where(diff >= 0.0, lgf[:, lo:lo + 1], -lgb[:, lo:lo + 1])
            dm_s[h] = jnp.exp(rate * diff)

    row_head = lax.broadcasted_iota(jnp.int32, (MIX, MIX), 0) // RET_HEAD_DIM
    col_head = lax.broadcasted_iota(jnp.int32, (MIX, MIX), 1) // RET_HEAD_DIM
    block_diag = row_head == col_head
    tn_dims = (((0,), (0,)), ((), ()))

    def advance(state, c, key_decay, chunk_decay):
        rows = pl.ds(c * c_len, c_len)
        kd = (k_ref[0, rows, :].astype(F32) * key_decay).astype(BF16)
        upd = lax.dot_general(kd, v_ref[0, rows, :], tn_dims, preferred_element_type=F32)
        return state * chunk_decay + jnp.where(block_diag, upd, 0.0)

    fwd, bwd = s0_ref[0, 0], s0_ref[0, 1]
    for i in range(nc):
        fill()
        sf_s[i] = fwd.astype(BF16)
        sb_s[nc - 1 - i] = bwd.astype(BF16)
        fwd = advance(fwd, i, dec_s[1], dec_s[4])
        bwd = advance(bwd, nc - 1 - i, dec_s[3], dec_s[5])
    sout_ref[0, 0] = fwd
    sout_ref[0, 1] = bwd


def _retention_step(q_ref, k_ref, v_ref, g_ref, rd_ref, s0_ref, avg_ref, o_ref, sout_ref,
                    sf_s, sb_s, dm_s, dec_s, *, n, fill):
    c_len = RET_CHUNK
    nc = n // c_len
    head = lax.broadcasted_iota(jnp.int32, (c_len, MIX), 1) // RET_HEAD_DIM
    _retention_states(k_ref, v_ref, rd_ref, s0_ref, sout_ref, sf_s, sb_s, dm_s, dec_s, n=n,
                      fill=fill)

    nt_dims = (((1,), (1,)), ((), ()))
    avg = avg_ref[...]
    for c0 in range(0, nc, RET_GROUP):
        group = range(c0, min(c0 + RET_GROUP, nc))
        scores, cross = {}, {}
        for c in group:
            fill()
            rows = pl.ds(c * c_len, c_len)
            q, k = q_ref[0, rows, :], k_ref[0, rows, :]
            qf = q.astype(F32)
            cross[c] = (_dot((qf * dec_s[0]).astype(BF16), sf_s[c])
                        + _dot((qf * dec_s[2]).astype(BF16), sb_s[c]))
            fill()
            scores[c] = [lax.dot_general(jnp.where(head == h, q, jnp.zeros_like(q)), k, nt_dims,
                                         preferred_element_type=F32) for h in range(RET_HEADS)]
        ys = {}
        for c in group:
            v = v_ref[0, pl.ds(c * c_len, c_len), :]
            probs = [(scores[c][h] * dm_s[h]).astype(BF16) for h in range(RET_HEADS)]
            vals = [jnp.where(head == h, v, jnp.zeros_like(v)) for h in range(RET_HEADS)]
            ys[c] = cross[c] + _dot(jnp.concatenate(probs, axis=1), jnp.concatenate(vals, axis=0))
        for c in group:
            fill()
            rows = pl.ds(c * c_len, c_len)
            d = ys[c] - _dot(ys[c].astype(BF16), avg)
            yn = d * lax.rsqrt(_dot((d * d).astype(BF16), avg) + EPS)
            o_ref[0, rows, :] = (yn * _silu(g_ref[0, rows, :].astype(F32))).astype(BF16)


def _padded_rows(ref, r0, rows, n):
    zeros = jnp.zeros((SEQ_PAD, MIX), ref.dtype)
    top = zeros if r0 == 0 else ref[0, r0 - SEQ_PAD:r0, :]
    bot = zeros if r0 + rows == n else ref[0, r0 + rows:r0 + rows + SEQ_PAD, :]
    return jnp.concatenate([top, ref[0, r0:r0 + rows, :], bot], axis=0).astype(F32)


def _shift_rows(t, s):
    return pltpu.roll(t, s % t.shape[0], 0)


def _conv_step(cb_ref, cg_ref, cx_ref, cw_ref, conv_ref, *, n, fill):
    rows = min(n, CONV_ROWS)
    main = slice(SEQ_PAD, SEQ_PAD + rows)
    cw = cw_ref[0]
    for r0 in range(0, n, rows):
        fill()
        u = _padded_rows(cg_ref, r0, rows, n) * _padded_rows(cx_ref, r0, rows, n)
        conv = _shift_rows(u, 1) * cw[0:1] + u * cw[1:2] + _shift_rows(u, -1) * cw[2:3]
        conv_ref[0, r0:r0 + rows, :] = (
            cb_ref[0, r0:r0 + rows, :].astype(F32) * conv[main]).astype(BF16)


def _pool_step(pu_ref, pw_ref, ps_ref, norm_ref, pool_ref, *, n, fill):
    rows = min(n, CONV_ROWS)
    main = slice(SEQ_PAD, SEQ_PAD + rows)
    in_first_group = lax.broadcasted_iota(jnp.int32, (rows, LANES), 1) < POOL_GROUP
    for r0 in range(0, n, rows):
        fill()
        e_all = _padded_rows(pu_ref, r0, rows, n)
        halves = []
        for half in range(MIX // LANES):
            e = e_all[:, half * LANES:(half + 1) * LANES]
            c = e + _shift_rows(e, 1)
            sums = [c]
            for w in POOL_WINDOWS[1:2 * half + 2]:
                c = _shift_rows(c, w // 4) + _shift_rows(c, -(w // 4))
                sums.append(c)
            halves.append(jnp.where(in_first_group, sums[-2][main], sums[-1][main]))
        win = jnp.concatenate(halves, axis=1)
        mean = win * norm_ref[1, 0:1]
        if r0 == 0:
            mean = jnp.concatenate([win[:EDGE_ROWS] * norm_ref[0], mean[EDGE_ROWS:]], axis=0)
        if r0 + rows == n:
            mean = jnp.concatenate([mean[:-EDGE_ROWS], win[-EDGE_ROWS:] * norm_ref[2]], axis=0)
        pooled = (mean - e_all[main]).astype(BF16)
        pool_ref[0, r0:r0 + rows, :] = (_dot(pooled, pw_ref[0]) * ps_ref[0]).astype(BF16)


def _times_const(z, w):
    re, im = z

    def lin(a, ca, b, cb):
        acc = None
        for t, coef in ((a, ca), (b, cb)):
            if abs(coef) < 1e-12:
                continue
            unit = abs(abs(coef) - 1.0) < 1e-12
            term = t if unit else t * abs(coef)
            if acc is None:
                acc = term if coef > 0 else -term
            else:
                acc = acc + term if coef > 0 else acc - term
        return acc

    return lin(re, w.real, im, -w.imag), lin(re, w.imag, im, w.real)


def _dft_blocks(xs):
    count = len(xs)
    if count == 1:
        return xs
    even, odd = _dft_blocks(xs[0::2]), _dft_blocks(xs[1::2])
    out = [None] * count
    for k in range(count // 2):
        t = _times_const(odd[k], cmath.exp(-2j * math.pi * k / count))
        out[k] = (even[k][0] + t[0], even[k][1] + t[1])
        out[k + count // 2] = (even[k][0] - t[0], even[k][1] - t[1])
    return out


def _fourier_step(u_ref, cc_ref, ssn_ref, tw_ref, dft_ref, o_ref, spec_s, *, n, fill):
    m = DFT_BLOCK
    r_cnt = n // m
    scale = 1.0 / math.sqrt(n * FFT_GROUP_DIM)
    blocks = []
    for r in range(r_cnt):
        u = u_ref[0, r * m:(r + 1) * m, :]
        blocks.append((_dot(u, cc_ref[...]), _dot(u, ssn_ref[...])))
        fill()
    spectra = _dft_blocks(blocks)
    for k1 in range(r_cnt):
        fill()
        re, im = spectra[k1]
        if k1 > 0:
            tc = jnp.concatenate([tw_ref[0, k1]] * (MIX // LANES), axis=1)
            ts = jnp.concatenate([tw_ref[1, k1]] * (MIX // LANES), axis=1)
            re, im = re * tc + im * ts, im * tc - re * ts
        fill()
        rhs = jnp.concatenate([re.astype(BF16), im.astype(BF16)], axis=0)
        block = _dot(dft_ref[...], rhs) * scale
        fill()
        for half in range(MIX // LANES):
            spec_s[half, pl.ds(k1, m, stride=r_cnt), :] = block[:, half * LANES:(half + 1) * LANES]
    o_ref[0] = jnp.concatenate([spec_s[half] for half in range(MIX // LANES)],
                               axis=1).astype(BF16)


def _mixer_kernel(h_ref, wg_ref, a_ref, b_ref, c_ref, d_ref, rd_ref, s0_ref, avg_ref,
                  cw_ref, pw_ref, ps_ref, norm_ref, cc_ref, ssn_ref, tw_ref, dft_ref,
                  pg_ref, ret_ref, sout_ref, conv_ref, pool_ref, four_ref,
                  sf_s, sb_s, dm_s, dec_s, spec_s, *, n):
    step = pl.program_id(1)

    def gate_filler(j, row_chunk):
        rows = min(n, row_chunk)
        todo = [(r0, c0) for r0 in range(0, n, rows) for c0 in range(0, D_MODEL, GATE_COLS)]
        calls_per_piece = max(1, (n // CONV_ROWS) // len(todo))
        pieces = iter(todo)
        calls = iter(range(n))

        def emit():
            r0, c0 = next(pieces, (None, None))
            if r0 is not None:
                w = wg_ref[0, :, j * D_MODEL + c0:j * D_MODEL + c0 + GATE_COLS]
                gate = _sigmoid(_dot(h_ref[0, r0:r0 + rows, :], w))
                pg_ref[0, r0:r0 + rows, c0:c0 + GATE_COLS] = gate.astype(BF16)

        def fill():
            if next(calls) % calls_per_piece == 0:
                emit()

        def drain():
            for _ in todo:
                emit()

        return fill, drain

    @pl.when(step == 0)
    def _():
        fill, drain = gate_filler(0, GATE_ROWS_MXU_STEP)
        _retention_step(a_ref, b_ref, c_ref, d_ref, rd_ref, s0_ref, avg_ref, ret_ref, sout_ref,
                        sf_s, sb_s, dm_s, dec_s, n=n, fill=fill)
        drain()

    @pl.when(step == 1)
    def _():
        fill, drain = gate_filler(1, GATE_ROWS_VPU_STEP)
        _conv_step(a_ref, b_ref, c_ref, cw_ref, conv_ref, n=n, fill=fill)
        drain()

    @pl.when(step == 2)
    def _():
        fill, drain = gate_filler(2, GATE_ROWS_MXU_STEP)
        _fourier_step(a_ref, cc_ref, ssn_ref, tw_ref, dft_ref, four_ref, spec_s, n=n, fill=fill)
        drain()

    @pl.when(step == 3)
    def _():
        fill, drain = gate_filler(3, GATE_ROWS_VPU_STEP)
        _pool_step(d_ref, pw_ref, ps_ref, norm_ref, pool_ref, n=n, fill=fill)
        drain()


_SLOT_COLUMNS = ((0, 4, 7, 7), (1, 5, 5, 5), (2, 6, 6, 6), (3, 3, 3, 8))


def _mixer(h, pa, wg, rd_lane, s0, avg, conv_w, pool_bd, pool_scale, pool_norm,
           cc, ssn, tw, dft, layer):
    b, n, _ = pa.shape
    nc = n // RET_CHUNK

    def slot(cols):
        def index(i, j):
            col = jnp.where(j == 0, cols[0],
                            jnp.where(j == 1, cols[1], jnp.where(j == 2, cols[2], cols[3])))
            return (i, 0, col)
        return pl.BlockSpec((1, n, MIX), index)

    seq_out = pl.BlockSpec((1, n, MIX), lambda i, j: (i, 0, 0))
    state = pl.BlockSpec((1, 2, MIX, MIX), lambda i, j: (i, 0, 0, 0))
    return pl.pallas_call(
        functools.partial(_mixer_kernel, n=n),
        out_shape=(jax.ShapeDtypeStruct((b, n, PG_WIDTH), BF16),
                   jax.ShapeDtypeStruct((b, n, MIX), BF16),
                   jax.ShapeDtypeStruct((b, 2, MIX, MIX), F32),
                   jax.ShapeDtypeStruct((b, n, MIX), BF16),
                   jax.ShapeDtypeStruct((b, n, MIX), BF16),
                   jax.ShapeDtypeStruct((b, n, MIX), BF16)),
        grid=(b, 4),
        in_specs=[pl.BlockSpec((1, n, D_MODEL), lambda i, j: (i, 0, 0)),
                  _layer_spec(wg.shape, 0),
                  slot(_SLOT_COLUMNS[0]), slot(_SLOT_COLUMNS[1]),
                  slot(_SLOT_COLUMNS[2]), slot(_SLOT_COLUMNS[3]),
                  _layer_spec(rd_lane.shape, layer), state, _const_spec(avg.shape),
                  _layer_spec(conv_w.shape, layer), _layer_spec(pool_bd.shape, layer),
                  _layer_spec(pool_scale.shape, layer), _const_spec(pool_norm.shape),
                  _const_spec(cc.shape), _const_spec(ssn.shape), _const_spec(tw.shape),
                  _const_spec(dft.shape)],
        out_specs=(pl.BlockSpec((1, n, D_MODEL), lambda i, j: (i, 0, j)),
                   seq_out, state, seq_out, seq_out, seq_out),
        scratch_shapes=[pltpu.VMEM((nc, MIX, MIX), BF16), pltpu.VMEM((nc, MIX, MIX), BF16),
                        pltpu.VMEM((RET_HEADS, RET_CHUNK, RET_CHUNK), F32),
                        pltpu.VMEM((6, RET_CHUNK, MIX), F32),
                        pltpu.VMEM((MIX // LANES, n, LANES), F32)],
        compiler_params=_params("arbitrary", "arbitrary"),
        name="mixer",
    )(h, wg, pa, pa, pa, pa, rd_lane, s0, avg, conv_w, pool_bd, pool_scale, pool_norm,
      cc, ssn, tw, dft)


def _state_kernel(k_ref, v_ref, rd_ref, s0_ref, sout_ref, sf_s, sb_s, dm_s, dec_s, *, n):
    _retention_states(k_ref, v_ref, rd_ref, s0_ref, sout_ref, sf_s, sb_s, dm_s, dec_s, n=n)


def _final_states(pa, rd_lane, s0, layer):
    b, n, _ = pa.shape
    nc = n // RET_CHUNK
    state = pl.BlockSpec((1, 2, MIX, MIX), lambda i: (i, 0, 0, 0))
    return pl.pallas_call(
        functools.partial(_state_kernel, n=n),
        out_shape=jax.ShapeDtypeStruct((b, 2, MIX, MIX), F32),
        grid=(b,),
        in_specs=[pl.BlockSpec((1, n, MIX), lambda i: (i, 0, 1)),
                  pl.BlockSpec((1, n, MIX), lambda i: (i, 0, 2)),
                  _layer_spec(rd_lane.shape, layer), state],
        out_specs=state,
        scratch_shapes=[pltpu.VMEM((nc, MIX, MIX), BF16), pltpu.VMEM((nc, MIX, MIX), BF16),
                        pltpu.VMEM((RET_HEADS, RET_CHUNK, RET_CHUNK), F32),
                        pltpu.VMEM((6, RET_CHUNK, MIX), F32)],
        compiler_params=_params("arbitrary"),
        name="final_states",
    )(pa, pa, rd_lane, s0)


FFN_SPLITS = (0, 768, 1536, 2304, D_FF)
POST_ROWS = 256


def _post_kernel(b0_ref, b1_ref, b2_ref, b3_ref, pg_ref, x_ref, mod_ref, g_ref,
                 wb_ref, wo_ref, wup_ref, wdn_ref, *rest):
    if len(rest) == 1:
        o_ref, = rest
        cast_jobs = []
    else:
        o_ref = rest[PREP_INPUTS]
        prep_refs = rest[:PREP_INPUTS] + rest[PREP_INPUTS + 1:]
        _prep_init(prep_refs[PREP_INPUTS - 1], prep_refs[-1],
                   (pl.program_id(0) == 0) & (pl.program_id(1) == 0))
        cast_jobs = [functools.partial(_prep_slabs, *prep_refs)]
    mod = mod_ref[0, 0]
    g = g_ref[0]

    def sub_tile(rows):
        merged = None
        for i, br in enumerate((b0_ref, b1_ref, b2_ref, b3_ref)):
            gate = pg_ref[0, rows, i * D_MODEL:(i + 1) * D_MODEL].astype(F32)
            t = gate * _dot(br[0, rows, :], wb_ref[0, i])
            merged = t if merged is None else merged + t
            if i % 2:
                yield "mix"
        mix = _dot(merged.astype(BF16), wo_ref[0])
        x1 = x_ref[0, rows, :] + _rms(mix, g[1:2] * mod[2:3])
        yield "mix"
        h = (_rms(x1, g[2:3] * (1.0 + mod[4:5])) + mod[3:4]).astype(BF16)
        yield "mix"
        ffn = None
        for lo, hi in zip(FFN_SPLITS[:-1], FFN_SPLITS[1:]):
            a = _dot(h, wup_ref[0, :, lo:hi])
            u = _dot(h, wup_ref[0, :, D_FF + lo:D_FF + hi])
            t = _dot((_silu(a) * u).astype(BF16), wdn_ref[0, lo:hi, :])
            ffn = t if ffn is None else ffn + t
            yield "ffn"
        o_ref[0, rows, :] = x1 + _rms(ffn, g[3:4] * mod[5:6])
        yield "done"

    tm = o_ref.shape[1]
    tiles = [sub_tile(pl.ds(r0, POST_ROWS)) for r0 in range(0, tm, POST_ROWS)]
    mix_pieces = ffn_chunks = len(FFN_SPLITS) - 1
    for _ in range(mix_pieces):
        next(tiles[0])
    for idx, cur in enumerate(tiles):
        for chunk in range(ffn_chunks):
            next(cur)
            if chunk == 0 and idx > 0:
                next(tiles[idx - 1])
            if idx + 1 < len(tiles):
                next(tiles[idx + 1])
            elif chunk >= 1 and cast_jobs:
                cast_jobs.pop()()
    next(tiles[-1])


def _post(branches, pg, x, mod, norm_g, weights, layer, mod_row, tm, cast_next=None):
    b, n, _ = x.shape
    tiles = n // tm
    br = pl.BlockSpec((1, tm, MIX), lambda i, j: (i, j, 0))
    in_specs = [br, br, br, br,
                pl.BlockSpec((1, tm, PG_WIDTH), lambda i, j: (i, j, 0)),
                pl.BlockSpec((1, tm, D_MODEL), lambda i, j: (i, j, 0)),
                _mod_spec(mod_row),
                _layer_spec(norm_g.shape, layer)]
    in_specs += [_layer_spec(w.shape, 0) for w in weights]
    out_shape = [jax.ShapeDtypeStruct((b, n, D_MODEL), F32)]
    out_specs = [pl.BlockSpec((1, tm, D_MODEL), lambda i, j: (i, j, 0))]
    operands = [*branches, pg, x, mod, norm_g, *weights]
    if cast_next is not None:
        cast = _cast_plan(cast_next, layer + 1, b * tiles, lambda i, j: i * tiles + j)
        operands += cast.operands
        in_specs += cast.in_specs
        out_shape += cast.out_shape
        out_specs += cast.out_specs
    out = pl.pallas_call(
        _post_kernel,
        out_shape=tuple(out_shape),
        grid=(b, tiles),
        in_specs=in_specs,
        out_specs=tuple(out_specs),
        compiler_params=_params("arbitrary", "arbitrary"),
        name="merge_ffn",
    )(*operands)
    if cast_next is None:
        return out[0]
    return (out[0],) + _layer_params(out[1:])


class _CastPlan(NamedTuple):
    operands: list
    in_specs: list
    out_shape: list
    out_specs: list


def _cast_plan(stacked, layer, steps, step_of):
    w_in, w_branch, w_o, w_up, w_down, c_t, w_mod, b_mod = stacked
    sources = (w_in, w_branch.reshape(DEPTH, N_BRANCH * MIX, D_MODEL), w_o, w_up, w_down)
    plan = _CastPlan([], [], [], [])
    for src in sources:
        rows, width = src.shape[1:]
        share = 1 if (rows // steps) % 16 == 0 else 2
        height = rows * share // steps

        def src_index(*idx, share=share):
            return (layer, step_of(*idx) // share, 0)

        def dst_index(*idx, share=share):
            return (0, step_of(*idx) // share, 0)

        plan.operands.append(src)
        plan.in_specs.append(pl.BlockSpec((1, height, width), src_index))
        for out_width in ((PA_WIDTH, PG_WIDTH) if src is w_in else (width,)):
            plan.out_shape.append(jax.ShapeDtypeStruct((1, rows, out_width), BF16))
            plan.out_specs.append(pl.BlockSpec((1, height, out_width), dst_index))

    slab = D_MODEL // steps
    n_out = w_mod.shape[-1]
    plan.operands.extend([c_t, w_mod, b_mod.reshape(DEPTH, 1, n_out)])
    plan.in_specs.extend([
        pl.BlockSpec((slab, MOD_ROWS), lambda *idx: (step_of(*idx), 0)),
        pl.BlockSpec((1, slab, n_out), lambda *idx: (layer, step_of(*idx), 0)),
        pl.BlockSpec((1, 1, n_out), lambda *idx: (layer, 0, 0))])
    plan.out_shape.append(jax.ShapeDtypeStruct((1, MOD_ROWS, n_out), F32))
    plan.out_specs.append(pl.BlockSpec((1, MOD_ROWS, n_out), lambda *idx: (0, 0, 0)))
    return plan


PREP_INPUTS = 8


def _prep_init(bm_ref, mod_o, first):
    @pl.when(first)
    def _():
        mod_o[0] = jnp.broadcast_to(bm_ref[0], mod_o.shape[1:])


def _prep_slabs(win_f, wb_f, wo_f, wup_f, wdn_f, ct_ref, wm_ref, bm_ref,
                wa_o, wg_o, wb_o, wo_o, wup_o, wdn_o, mod_o):
    wa_o[0] = win_f[0, :, :PA_WIDTH].astype(BF16)
    wg_o[0] = win_f[0, :, PA_WIDTH:].astype(BF16)
    for src, dst in ((wb_f, wb_o), (wo_f, wo_o), (wup_f, wup_o), (wdn_f, wdn_o)):
        dst[0] = src[0].astype(BF16)
    a = _silu(ct_ref[...]).astype(BF16)
    mod_o[0] += lax.dot_general(a, wm_ref[0].astype(BF16), (((0,), (0,)), ((), ())),
                                preferred_element_type=F32)


def _layer_params(prep_outputs):
    wa, wg, wb, wo, wup, wdn, mod = prep_outputs
    return ((wa, wg, wb.reshape(1, N_BRANCH, MIX, D_MODEL), wo, wup, wdn),
            mod.reshape(1, MOD_ROWS, 6, D_MODEL))


def _prep_kernel(*refs):
    _prep_init(refs[PREP_INPUTS - 1], refs[-1], pl.program_id(0) == 0)
    _prep_slabs(*refs)


def _prep_layer(stacked, layer):
    steps = 32
    plan = _cast_plan(stacked, layer, steps, lambda s: s)
    out = pl.pallas_call(
        _prep_kernel,
        out_shape=tuple(plan.out_shape),
        grid=(steps,),
        in_specs=plan.in_specs,
        out_specs=tuple(plan.out_specs),
        compiler_params=_params("arbitrary"),
        name="prep_layer",
    )(*plan.operands)
    return _layer_params(out)


def _rope_tables(n):
    rows = np.repeat(np.arange(n // GRID_W, dtype=np.float64), GRID_W)
    cols = np.tile(np.arange(GRID_W, dtype=np.float64), n // GRID_W)
    lane = np.arange(MIX)
    in_head = lane % RET_HEAD_DIM
    pos = np.where((in_head < RET_HEAD_DIM // 2)[None, :], rows[:, None], cols[:, None])
    quarter = RET_HEAD_DIM // 4
    freq = ROPE_BASE ** (-np.arange(quarter, dtype=np.float64) / quarter)
    ang = pos * freq[lane % quarter][None, :]
    sign = np.where((lane % (2 * quarter)) < quarter, -1.0, 1.0)[None, :]
    return np.cos(ang).astype(np.float32), (np.sin(ang) * sign).astype(np.float32)


def _dft_tables(n):
    m = DFT_BLOCK
    r_cnt = n // m
    k = np.arange(m, dtype=np.int64)
    ang = 2.0 * np.pi * ((k[:, None] * k[None, :]) % m) / m
    dft = np.concatenate([np.cos(ang), np.sin(ang)], axis=1).astype(np.float32)
    angt = 2.0 * np.pi * (np.arange(r_cnt)[:, None] * k[None, :]) / n
    tw = np.stack([np.cos(angt), np.sin(angt)])[..., None].repeat(LANES, axis=-1).astype(np.float32)
    ch = np.arange(FFT_GROUP_DIM, dtype=np.int64)
    angc = 2.0 * np.pi * ((ch[:, None] * ch[None, :]) % FFT_GROUP_DIM) / FFT_GROUP_DIM
    eye = np.eye(FFT_GROUPS)
    cc = np.kron(eye, np.cos(angc)).astype(np.float32)
    ssn = np.kron(eye, -np.sin(angc)).astype(np.float32)
    return dft, tw, cc, ssn


def _pool_norm(n):
    t = np.arange(n)
    cols = []
    for w in POOL_WINDOWS:
        cnt = np.clip(t - w // 2 + w, 0, n) - np.clip(t - w // 2, 0, n)
        inv = 1.0 / cnt
        assert np.all(cnt[EDGE_ROWS:n - EDGE_ROWS] == w)
        col = np.stack([inv[:EDGE_ROWS], np.full(EDGE_ROWS, 1.0 / w), inv[n - EDGE_ROWS:]])
        cols.append(np.repeat(col[:, :, None], POOL_GROUP, axis=2))
    return np.concatenate(cols, axis=2).astype(np.float32)


def _block_diag(w):
    nl, g, d, _ = w.shape
    eye = jnp.eye(g, dtype=w.dtype)
    return (eye[None, :, None, :, None] * w[:, :, :, None, :]).reshape(nl, g * d, g * d)


def kernel(x, c, ctx, c_ctx, w_mod, b_mod, norm_g, w_in, ret_decay, conv_w, pool_w, pool_scale,
           w_branch, w_o, ffn_w_up, ffn_w_down):
    b, n, _ = x.shape
    n_ctx = ctx.shape[1]

    c_t = jnp.concatenate(
        [c, c_ctx[None, :], jnp.zeros((MOD_ROWS - b - 1, D_MODEL), F32)], axis=0).T
    ctx_row = b

    cos_l, sin_l = (jnp.asarray(t) for t in _rope_tables(n))
    cos_c, sin_c = jnp.ones((b * n_ctx, MIX), F32), jnp.zeros((b * n_ctx, MIX), F32)
    ctx = ctx.reshape(1, b * n_ctx, D_MODEL)

    def per_sample(t):
        return t.reshape(b, n_ctx, t.shape[-1])

    def flat(t):
        return t.reshape(1, b * n_ctx, t.shape[-1]) if t.shape[1] == n_ctx else t

    def fourier_tables(length):
        dft, tw, cc, ssn = _dft_tables(length)
        return (jnp.asarray(cc).astype(BF16), jnp.asarray(ssn).astype(BF16), jnp.asarray(tw),
                jnp.asarray(dft).astype(BF16))

    tabs_l, tabs_c = fourier_tables(n), fourier_tables(n_ctx)
    norm_l, norm_c = jnp.asarray(_pool_norm(n)), jnp.asarray(_pool_norm(n_ctx))
    avg = jnp.asarray(np.kron(np.eye(RET_HEADS), np.full((RET_HEAD_DIM, RET_HEAD_DIM),
                                                         1.0 / RET_HEAD_DIM)), F32).astype(BF16)
    s_zero = jnp.zeros((b, 2, MIX, MIX), F32)

    w_stacked = (w_in, w_branch, w_o, ffn_w_up, ffn_w_down, c_t, w_mod, b_mod)
    w_layer, mod = _prep_layer(w_stacked, 0)
    pool_bd = _block_diag(pool_w).astype(BF16)
    rd_lane = jnp.repeat(ret_decay.astype(F32), RET_HEAD_DIM, axis=2)
    ps = pool_scale.reshape(DEPTH, 1, MIX)

    for l in range(DEPTH):
        need_ctx = l < DEPTH - 1
        wa, wg = w_layer[:2]
        if need_ctx:
            h_c, pa_c = _in_proj(ctx, mod, norm_g, wa, cos_c, sin_c, l, ctx_row, tm=CTX_TILE)
            mixed_c = _mixer(per_sample(h_c), per_sample(pa_c), wg, rd_lane, s_zero, avg, conv_w,
                             pool_bd, ps, norm_c, *tabs_c, l)
            s_ctx = mixed_c[2]
            pg_c, ret_c, _, conv_c, pool_c, four_c = (flat(t) for t in mixed_c)
        else:
            _, qkv_c = _in_proj(ctx, mod, norm_g, wa, cos_c, sin_c, l, ctx_row, tm=CTX_TILE,
                                pa_width=3 * MIX)
            s_ctx = _final_states(per_sample(qkv_c), rd_lane, s_zero, l)

        h_l, pa_l = _in_proj(x, mod, norm_g, wa, cos_l, sin_l, l, None, tm=1024)
        pg_l, ret_l, _, conv_l, pool_l, four_l = _mixer(
            h_l, pa_l, wg, rd_lane, s_ctx, avg, conv_w, pool_bd, ps, norm_l, *tabs_l, l)
        if need_ctx:
            ctx = _post((ret_c, conv_c, four_c, pool_c), pg_c, ctx, mod, norm_g, w_layer[2:],
                        l, ctx_row, tm=512)
            x, w_layer, mod = _post((ret_l, conv_l, four_l, pool_l), pg_l, x, mod, norm_g,
                                    w_layer[2:], l, None, tm=512, cast_next=w_stacked)
        else:
            x = _post((ret_l, conv_l, four_l, pool_l), pg_l, x, mod, norm_g, w_layer[2:],
                      l, None, tm=512)
    return x
```

```python
import cmath
import functools
import math
from typing import NamedTuple

import numpy as np
import jax
import jax.numpy as jnp
from jax import lax
from jax.experimental import pallas as pl
from jax.experimental.pallas import tpu as pltpu

D_MODEL = 1024
DEPTH = 4
GRID_W = 64
MIX = 256
N_BRANCH = 4
RET_HEADS = 4
RET_HEAD_DIM = MIX // RET_HEADS
POOL_WINDOWS = (2, 4, 8, 16)
POOL_GROUP = MIX // len(POOL_WINDOWS)
FFT_GROUPS = 4
FFT_GROUP_DIM = MIX // FFT_GROUPS
D_FF = 2816
PA_WIDTH = 9 * MIX
PG_WIDTH = N_BRANCH * D_MODEL
ROPE_BASE = 10000.0
EPS = 1e-6

LANES = 128
RET_CHUNK = 256
RET_GROUP = 4
DFT_BLOCK = 256
SEQ_PAD = 16
EDGE_ROWS = 8
GATE_ROWS_MXU_STEP = 512
GATE_ROWS_VPU_STEP = 256
GATE_COLS = 256
CONV_ROWS = 64
CTX_TILE = 1024
IN_ROWS = 256
MOD_ROWS = 16
VMEM_LIMIT = 60 * 1024 * 1024

F32 = jnp.float32
BF16 = jnp.bfloat16


def _dot(a, b):
    return jnp.dot(a, b, preferred_element_type=F32)


def _rms(xf, g):
    ms = jnp.mean(xf * xf, axis=-1, keepdims=True)
    return xf * lax.rsqrt(ms + EPS) * g


def _sigmoid(x):
    return 0.5 * jnp.tanh(0.5 * x) + 0.5


def _silu(x):
    return x * _sigmoid(x)


def _const_spec(shape):
    zeros = (0,) * len(shape)
    return pl.BlockSpec(shape, lambda *_: zeros, pipeline_mode=pl.Buffered(1))


def _layer_spec(shape, layer):
    tail = (0,) * (len(shape) - 1)
    return pl.BlockSpec((1,) + tuple(shape[1:]), lambda *_: (layer,) + tail,
                        pipeline_mode=pl.Buffered(1))


def _mod_spec(row):
    return pl.BlockSpec((1, 1, 6, D_MODEL),
                        lambda i, j: (0, i if row is None else row, 0, 0))


def _params(*sem):
    return pltpu.CompilerParams(dimension_semantics=sem, vmem_limit_bytes=VMEM_LIMIT)


def _in_kernel(x_ref, mod_ref, g_ref, w_ref, cos_ref, sin_ref, h_ref, pa_ref):
    mod = mod_ref[0, 0]
    gain = g_ref[0, 0:1] * (1.0 + mod[1:2])
    tm = x_ref.shape[1]
    sub = min(tm, IN_ROWS)
    first_of_pair = (lax.broadcasted_iota(jnp.int32, (sub, MIX), 1) & 16) == 0

    def sub_tile(rows):
        hb = (_rms(x_ref[0, rows, :], gain) + mod[0:1]).astype(BF16)
        h_ref[0, rows, :] = hb
        yield
        cos, sin = cos_ref[rows, :], sin_ref[rows, :]

        def rope(t, scale):
            partner = jnp.where(first_of_pair, pltpu.roll(t, MIX - 16, 1), pltpu.roll(t, 16, 1))
            return (t * cos + partner * sin) * scale

        for g0 in range(0, pa_ref.shape[0], 3):
            res = _dot(hb, w_ref[0, :, g0 * MIX:(g0 + 3) * MIX])
            for g in range(3):
                t = res[:, g * MIX:(g + 1) * MIX]
                if g0 + g == 0:
                    t = rope(t, RET_HEAD_DIM ** -0.5)
                elif g0 + g == 1:
                    t = rope(t, 1.0)
                pa_ref[g0 + g, rows, :] = t.astype(BF16)
            yield

    tiles = [sub_tile(pl.ds(r0, sub)) for r0 in range(0, tm, sub)]
    next(tiles[0])
    for idx, cur in enumerate(tiles):
        next(cur)
        if idx + 1 < len(tiles):
            next(tiles[idx + 1])
        for _ in cur:
            pass


def _in_proj(x, mod, norm_g, w_in, cos, sin, layer, mod_row, tm, pa_width=PA_WIDTH):
    b, n, _ = x.shape
    tiles = n // tm
    return pl.pallas_call(
        _in_kernel,
        out_shape=(jax.ShapeDtypeStruct((b, n, D_MODEL), BF16),
                   jax.ShapeDtypeStruct((pa_width // MIX, b * n, MIX), BF16)),
        grid=(b, tiles),
        in_specs=[
            pl.BlockSpec((1, tm, D_MODEL), lambda i, j: (i, j, 0)),
            _mod_spec(mod_row),
            _layer_spec(norm_g.shape, layer),
            pl.BlockSpec((1, D_MODEL, pa_width), lambda i, j: (0, 0, 0),
                         pipeline_mode=pl.Buffered(1)),
            pl.BlockSpec((tm, MIX), lambda i, j: (j, 0)),
            pl.BlockSpec((tm, MIX), lambda i, j: (j, 0)),
        ],
        out_specs=(pl.BlockSpec((1, tm, D_MODEL), lambda i, j: (i, j, 0)),
                   pl.BlockSpec((pa_width // MIX, tm, MIX), lambda i, j: (0, i * tiles + j, 0))),
        compiler_params=_params("parallel", "parallel"),
        name="in_proj",
    )(x, mod, norm_g, w_in, cos, sin)


def _retention_states(k_ref, v_ref, rd_ref, s0_ref, sout_ref, sf_s, sb_s, dm_s, dec_s, *, n,
                      fill=lambda: None):
    c_len = RET_CHUNK
    nc = n // c_len

    @pl.when(pl.program_id(0) == 0)
    def _decay_tables():
        rd = rd_ref[0]
        lg = jnp.minimum(rd, 0.0) - jnp.log1p(jnp.exp(-jnp.abs(rd)))
        lgf, lgb = lg[0:1], lg[1:2]
        idx = lax.broadcasted_iota(jnp.int32, (c_len, MIX), 0).astype(F32)
        dec_s[0] = jnp.exp(lgf * (idx + 1.0))
        dec_s[1] = jnp.exp(lgf * (c_len - 1.0 - idx))
        dec_s[2] = jnp.exp(lgb * (c_len - idx))
        dec_s[3] = jnp.exp(lgb * idx)
        dec_s[4] = jnp.exp(lgf * float(c_len)) + 0.0 * idx
        dec_s[5] = jnp.exp(lgb * float(c_len)) + 0.0 * idx
        ii = lax.broadcasted_iota(jnp.int32, (c_len, c_len), 0)
        jj = lax.broadcasted_iota(jnp.int32, (c_len, c_len), 1)
        diff = (ii - jj).astype(F32)
        for h in range(RET_HEADS):
            lo = h * RET_HEAD_DIM
            rate = jnp.where(diff >= 0.0, lgf[:, lo:lo + 1], -lgb[:, lo:lo + 1])
            dm_s[h] = jnp.exp(rate * diff)

    row_head = lax.broadcasted_iota(jnp.int32, (MIX, MIX), 0) // RET_HEAD_DIM
    col_head = lax.broadcasted_iota(jnp.int32, (MIX, MIX), 1) // RET_HEAD_DIM
    block_diag = row_head == col_head
    tn_dims = (((0,), (0,)), ((), ()))

    def advance(state, c, key_decay, chunk_decay):
        rows = pl.ds(c * c_len, c_len)
        kd = (k_ref[0, rows, :].astype(F32) * key_decay).astype(BF16)
        upd = lax.dot_general(kd, v_ref[0, rows, :], tn_dims, preferred_element_type=F32)
        return state * chunk_decay + jnp.where(block_diag, upd, 0.0)

    fwd, bwd = s0_ref[0, 0], s0_ref[0, 1]
    for i in range(nc):
        fill()
        sf_s[i] = fwd.astype(BF16)
        sb_s[nc - 1 - i] = bwd.astype(BF16)
        fwd = advance(fwd, i, dec_s[1], dec_s[4])
        bwd = advance(bwd, nc - 1 - i, dec_s[3], dec_s[5])
    sout_ref[0, 0] = fwd
    sout_ref[0, 1] = bwd


def _retention_step(q_ref, k_ref, v_ref, g_ref, rd_ref, s0_ref, avg_ref, o_ref, sout_ref,
                    sf_s, sb_s, dm_s, dec_s, *, n, fill):
    c_len = RET_CHUNK
    nc = n // c_len
    head = lax.broadcasted_iota(jnp.int32, (c_len, MIX), 1) // RET_HEAD_DIM
    _retention_states(k_ref, v_ref, rd_ref, s0_ref, sout_ref, sf_s, sb_s, dm_s, dec_s, n=n,
                      fill=fill)

    nt_dims = (((1,), (1,)), ((), ()))
    avg = avg_ref[...]
    for c0 in range(0, nc, RET_GROUP):
        group = range(c0, min(c0 + RET_GROUP, nc))
        scores, cross = {}, {}
        for c in group:
            fill()
            rows = pl.ds(c * c_len, c_len)
            q, k = q_ref[0, rows, :], k_ref[0, rows, :]
            qf = q.astype(F32)
            cross[c] = (_dot((qf * dec_s[0]).astype(BF16), sf_s[c])
                        + _dot((qf * dec_s[2]).astype(BF16), sb_s[c]))
            fill()
            scores[c] = [lax.dot_general(jnp.where(head == h, q, jnp.zeros_like(q)), k, nt_dims,
                                         preferred_element_type=F32) for h in range(RET_HEADS)]
        ys = {}
        for c in group:
            v = v_ref[0, pl.ds(c * c_len, c_len), :]
            probs = [(scores[c][h] * dm_s[h]).astype(BF16) for h in range(RET_HEADS)]
            vals = [jnp.where(head == h, v, jnp.zeros_like(v)) for h in range(RET_HEADS)]
            ys[c] = cross[c] + _dot(jnp.concatenate(probs, axis=1), jnp.concatenate(vals, axis=0))
        for c in group:
            fill()
            rows = pl.ds(c * c_len, c_len)
            d = ys[c] - _dot(ys[c].astype(BF16), avg)
            yn = d * lax.rsqrt(_dot((d * d).astype(BF16), avg) + EPS)
            o_ref[0, rows, :] = (yn * _silu(g_ref[0, rows, :].astype(F32))).astype(BF16)


def _padded_rows(ref, r0, rows, n):
    zeros = jnp.zeros((SEQ_PAD, MIX), ref.dtype)
    top = zeros if r0 == 0 else ref[0, r0 - SEQ_PAD:r0, :]
    bot = zeros if r0 + rows == n else ref[0, r0 + rows:r0 + rows + SEQ_PAD, :]
    return jnp.concatenate([top, ref[0, r0:r0 + rows, :], bot], axis=0).astype(F32)


def _shift_rows(t, s):
    return pltpu.roll(t, s % t.shape[0], 0)


def _conv_step(cb_ref, cg_ref, cx_ref, cw_ref, conv_ref, *, n, fill):
    rows = min(n, CONV_ROWS)
    main = slice(SEQ_PAD, SEQ_PAD + rows)
    cw = cw_ref[0]
    for r0 in range(0, n, rows):
        fill()
        u = _padded_rows(cg_ref, r0, rows, n) * _padded_rows(cx_ref, r0, rows, n)
        conv = _shift_rows(u, 1) * cw[0:1] + u * cw[1:2] + _shift_rows(u, -1) * cw[2:3]
        conv_ref[0, r0:r0 + rows, :] = (
            cb_ref[0, r0:r0 + rows, :].astype(F32) * conv[main]).astype(BF16)


def _pool_step(pu_ref, pw_ref, ps_ref, norm_ref, pool_ref, *, n, fill):
    rows = min(n, CONV_ROWS)
    main = slice(SEQ_PAD, SEQ_PAD + rows)
    in_first_group = lax.broadcasted_iota(jnp.int32, (rows, LANES), 1) < POOL_GROUP
    for r0 in range(0, n, rows):
        fill()
        e_all = _padded_rows(pu_ref, r0, rows, n)
        halves = []
        for half in range(MIX // LANES):
            e = e_all[:, half * LANES:(half + 1) * LANES]
            c = e + _shift_rows(e, 1)
            sums = [c]
            for w in POOL_WINDOWS[1:2 * half + 2]:
                c = _shift_rows(c, w // 4) + _shift_rows(c, -(w // 4))
                sums.append(c)
            halves.append(jnp.where(in_first_group, sums[-2][main], sums[-1][main]))
        win = jnp.concatenate(halves, axis=1)
        mean = win * norm_ref[1, 0:1]
        if r0 == 0:
            mean = jnp.concatenate([win[:EDGE_ROWS] * norm_ref[0], mean[EDGE_ROWS:]], axis=0)
        if r0 + rows == n:
            mean = jnp.concatenate([mean[:-EDGE_ROWS], win[-EDGE_ROWS:] * norm_ref[2]], axis=0)
        pooled = (mean - e_all[main]).astype(BF16)
        pool_ref[0, r0:r0 + rows, :] = (_dot(pooled, pw_ref[0]) * ps_ref[0]).astype(BF16)


def _times_const(z, w):
    re, im = z

    def lin(a, ca, b, cb):
        acc = None
        for t, coef in ((a, ca), (b, cb)):
            if abs(coef) < 1e-12:
                continue
            unit = abs(abs(coef) - 1.0) < 1e-12
            term = t if unit else t * abs(coef)
            if acc is None:
                acc = term if coef > 0 else -term
            else:
                acc = acc + term if coef > 0 else acc - term
        return acc

    return lin(re, w.real, im, -w.imag), lin(re, w.imag, im, w.real)


def _dft_blocks(xs):
    count = len(xs)
    if count == 1:
        return xs
    even, odd = _dft_blocks(xs[0::2]), _dft_blocks(xs[1::2])
    out = [None] * count
    for k in range(count // 2):
        t = _times_const(odd[k], cmath.exp(-2j * math.pi * k / count))
        out[k] = (even[k][0] + t[0], even[k][1] + t[1])
        out[k + count // 2] = (even[k][0] - t[0], even[k][1] - t[1])
    return out


def _fourier_step(u_ref, cc_ref, ssn_ref, tw_ref, dft_ref, o_ref, spec_s, *, n, fill):
    m = DFT_BLOCK
    r_cnt = n // m
    scale = 1.0 / math.sqrt(n * FFT_GROUP_DIM)
    blocks = []
    for r in range(r_cnt):
        u = u_ref[0, r * m:(r + 1) * m, :]
        blocks.append((_dot(u, cc_ref[...]), _dot(u, ssn_ref[...])))
        fill()
    spectra = _dft_blocks(blocks)
    for k1 in range(r_cnt):
        fill()
        re, im = spectra[k1]
        if k1 > 0:
            tc = jnp.concatenate([tw_ref[0, k1]] * (MIX // LANES), axis=1)
            ts = jnp.concatenate([tw_ref[1, k1]] * (MIX // LANES), axis=1)
            re, im = re * tc + im * ts, im * tc - re * ts
        fill()
        rhs = jnp.concatenate([re.astype(BF16), im.astype(BF16)], axis=0)
        block = _dot(dft_ref[...], rhs) * scale
        fill()
        for half in range(MIX // LANES):
            spec_s[half, pl.ds(k1, m, stride=r_cnt), :] = block[:, half * LANES:(half + 1) * LANES]
    o_ref[0] = jnp.concatenate([spec_s[half] for half in range(MIX // LANES)],
                               axis=1).astype(BF16)


def _mixer_kernel(h_ref, wg_ref, a_ref, b_ref, c_ref, d_ref, rd_ref, s0_ref, avg_ref,
                  cw_ref, pw_ref, ps_ref, norm_ref, cc_ref, ssn_ref, tw_ref, dft_ref,
                  pg_ref, ret_ref, sout_ref, conv_ref, pool_ref, four_ref,
                  sf_s, sb_s, dm_s, dec_s, spec_s, *, n):
    step = pl.program_id(1)

    def gate_filler(j, row_chunk):
        rows = min(n, row_chunk)
        todo = [(r0, c0) for r0 in range(0, n, rows) for c0 in range(0, D_MODEL, GATE_COLS)]
        calls_per_piece = max(1, (n // CONV_ROWS) // len(todo))
        pieces = iter(todo)
        calls = iter(range(n))

        def emit():
            r0, c0 = next(pieces, (None, None))
            if r0 is not None:
                w = wg_ref[0, :, j * D_MODEL + c0:j * D_MODEL + c0 + GATE_COLS]
                gate = _sigmoid(_dot(h_ref[0, r0:r0 + rows, :], w))
                pg_ref[0, r0:r0 + rows, c0:c0 + GATE_COLS] = gate.astype(BF16)

        def fill():
            if next(calls) % calls_per_piece == 0:
                emit()

        def drain():
            for _ in todo:
                emit()

        return fill, drain

    @pl.when(step == 0)
    def _():
        fill, drain = gate_filler(0, GATE_ROWS_MXU_STEP)
        _retention_step(a_ref, b_ref, c_ref, d_ref, rd_ref, s0_ref, avg_ref, ret_ref, sout_ref,
                        sf_s, sb_s, dm_s, dec_s, n=n, fill=fill)
        drain()

    @pl.when(step == 1)
    def _():
        fill, drain = gate_filler(1, GATE_ROWS_VPU_STEP)
        _conv_step(a_ref, b_ref, c_ref, cw_ref, conv_ref, n=n, fill=fill)
        drain()

    @pl.when(step == 2)
    def _():
        fill, drain = gate_filler(2, GATE_ROWS_MXU_STEP)
        _fourier_step(a_ref, cc_ref, ssn_ref, tw_ref, dft_ref, four_ref, spec_s, n=n, fill=fill)
        drain()

    @pl.when(step == 3)
    def _():
        fill, drain = gate_filler(3, GATE_ROWS_VPU_STEP)
        _pool_step(d_ref, pw_ref, ps_ref, norm_ref, pool_ref, n=n, fill=fill)
        drain()


_SLOT_COLUMNS = ((0, 4, 7, 7), (1, 5, 5, 5), (2, 6, 6, 6), (3, 3, 3, 8))


def _mixer(h, pa, wg, rd_lane, s0, avg, conv_w, pool_bd, pool_scale, pool_norm,
           cc, ssn, tw, dft, layer):
    b, n, _ = h.shape
    nc = n // RET_CHUNK

    def slot(cols):
        def index(i, j):
            col = jnp.where(j == 0, cols[0],
                            jnp.where(j == 1, cols[1], jnp.where(j == 2, cols[2], cols[3])))
            return (col, i, 0)
        return pl.BlockSpec((1, n, MIX), index)

    seq_out = pl.BlockSpec((1, n, MIX), lambda i, j: (i, 0, 0))
    state = pl.BlockSpec((1, 2, MIX, MIX), lambda i, j: (i, 0, 0, 0))
    return pl.pallas_call(
        functools.partial(_mixer_kernel, n=n),
        out_shape=(jax.ShapeDtypeStruct((b, n, PG_WIDTH), BF16),
                   jax.ShapeDtypeStruct((b, n, MIX), BF16),
                   jax.ShapeDtypeStruct((b, 2, MIX, MIX), F32),
                   jax.ShapeDtypeStruct((b, n, MIX), BF16),
                   jax.ShapeDtypeStruct((b, n, MIX), BF16),
                   jax.ShapeDtypeStruct((b, n, MIX), BF16)),
        grid=(b, 4),
        in_specs=[pl.BlockSpec((1, n, D_MODEL), lambda i, j: (i, 0, 0)),
                  _layer_spec(wg.shape, 0),
                  slot(_SLOT_COLUMNS[0]), slot(_SLOT_COLUMNS[1]),
                  slot(_SLOT_COLUMNS[2]), slot(_SLOT_COLUMNS[3]),
                  _layer_spec(rd_lane.shape, layer), state, _const_spec(avg.shape),
                  _layer_spec(conv_w.shape, layer), _layer_spec(pool_bd.shape, layer),
                  _layer_spec(pool_scale.shape, layer), _const_spec(pool_norm.shape),
                  _const_spec(cc.shape), _const_spec(ssn.shape), _const_spec(tw.shape),
                  _const_spec(dft.shape)],
        out_specs=(pl.BlockSpec((1, n, D_MODEL), lambda i, j: (i, 0, j)),
                   seq_out, state, seq_out, seq_out, seq_out),
        scratch_shapes=[pltpu.VMEM((nc, MIX, MIX), BF16), pltpu.VMEM((nc, MIX, MIX), BF16),
                        pltpu.VMEM((RET_HEADS, RET_CHUNK, RET_CHUNK), F32),
                        pltpu.VMEM((6, RET_CHUNK, MIX), F32),
                        pltpu.VMEM((MIX // LANES, n, LANES), F32)],
        compiler_params=_params("arbitrary", "arbitrary"),
        name="mixer",
    )(h, wg, pa, pa, pa, pa, rd_lane, s0, avg, conv_w, pool_bd, pool_scale, pool_norm,
      cc, ssn, tw, dft)


def _state_kernel(k_ref, v_ref, rd_ref, s0_ref, sout_ref, sf_s, sb_s, dm_s, dec_s, *, n):
    _retention_states(k_ref, v_ref, rd_ref, s0_ref, sout_ref, sf_s, sb_s, dm_s, dec_s, n=n)


def _final_states(pa, rd_lane, s0, layer):
    b = s0.shape[0]
    n = pa.shape[1] // b
    nc = n // RET_CHUNK
    state = pl.BlockSpec((1, 2, MIX, MIX), lambda i: (i, 0, 0, 0))
    return pl.pallas_call(
        functools.partial(_state_kernel, n=n),
        out_shape=jax.ShapeDtypeStruct((b, 2, MIX, MIX), F32),
        grid=(b,),
        in_specs=[pl.BlockSpec((1, n, MIX), lambda i: (1, i, 0)),
                  pl.BlockSpec((1, n, MIX), lambda i: (2, i, 0)),
                  _layer_spec(rd_lane.shape, layer), state],
        out_specs=state,
        scratch_shapes=[pltpu.VMEM((nc, MIX, MIX), BF16), pltpu.VMEM((nc, MIX, MIX), BF16),
                        pltpu.VMEM((RET_HEADS, RET_CHUNK, RET_CHUNK), F32),
                        pltpu.VMEM((6, RET_CHUNK, MIX), F32)],
        compiler_params=_params("arbitrary"),
        name="final_states",
    )(pa, pa, rd_lane, s0)


FFN_SPLITS = (0, 768, 1536, 2304, D_FF)
POST_ROWS = 256


def _post_kernel(b0_ref, b1_ref, b2_ref, b3_ref, pg_ref, x_ref, mod_ref, g_ref,
                 wb_ref, wo_ref, wup_ref, wdn_ref, *rest):
    if len(rest) == 1:
        o_ref, = rest
        cast_jobs = []
    else:
        o_ref = rest[PREP_INPUTS]
        prep_refs = rest[:PREP_INPUTS] + rest[PREP_INPUTS + 1:]
        _prep_init(prep_refs[PREP_INPUTS - 1], prep_refs[-1],
                   (pl.program_id(0) == 0) & (pl.program_id(1) == 0))
        cast_jobs = [functools.partial(_prep_slabs, *prep_refs)]
    mod = mod_ref[0, 0]
    g = g_ref[0]

    def sub_tile(rows):
        merged = None
        for i, br in enumerate((b0_ref, b1_ref, b2_ref, b3_ref)):
            gate = pg_ref[0, rows, i * D_MODEL:(i + 1) * D_MODEL].astype(F32)
            t = gate * _dot(br[0, rows, :], wb_ref[0, i])
            merged = t if merged is None else merged + t
            if i % 2:
                yield "mix"
        mix = _dot(merged.astype(BF16), wo_ref[0])
        x1 = x_ref[0, rows, :] + _rms(mix, g[1:2] * mod[2:3])
        yield "mix"
        h = (_rms(x1, g[2:3] * (1.0 + mod[4:5])) + mod[3:4]).astype(BF16)
        yield "mix"
        ffn = None
        for lo, hi in zip(FFN_SPLITS[:-1], FFN_SPLITS[1:]):
            a = _dot(h, wup_ref[0, :, lo:hi])
            u = _dot(h, wup_ref[0, :, D_FF + lo:D_FF + hi])
            t = _dot((_silu(a) * u).astype(BF16), wdn_ref[0, lo:hi, :])
            ffn = t if ffn is None else ffn + t
            yield "ffn"
        o_ref[0, rows, :] = x1 + _rms(ffn, g[3:4] * mod[5:6])
        yield "done"

    tm = o_ref.shape[1]
    tiles = [sub_tile(pl.ds(r0, POST_ROWS)) for r0 in range(0, tm, POST_ROWS)]
    mix_pieces = ffn_chunks = len(FFN_SPLITS) - 1
    for _ in range(mix_pieces):
        next(tiles[0])
    for idx, cur in enumerate(tiles):
        for chunk in range(ffn_chunks):
            next(cur)
            if chunk == 0 and idx > 0:
                next(tiles[idx - 1])
            if idx + 1 < len(tiles):
                next(tiles[idx + 1])
            elif chunk >= 1 and cast_jobs:
                cast_jobs.pop()()
    next(tiles[-1])


def _post(branches, pg, x, mod, norm_g, weights, layer, mod_row, tm, cast_next=None):
    b, n, _ = x.shape
    tiles = n // tm
    br = pl.BlockSpec((1, tm, MIX), lambda i, j: (i, j, 0))
    in_specs = [br, br, br, br,
                pl.BlockSpec((1, tm, PG_WIDTH), lambda i, j: (i, j, 0)),
                pl.BlockSpec((1, tm, D_MODEL), lambda i, j: (i, j, 0)),
                _mod_spec(mod_row),
                _layer_spec(norm_g.shape, layer)]
    in_specs += [_layer_spec(w.shape, 0) for w in weights]
    out_shape = [jax.ShapeDtypeStruct((b, n, D_MODEL), F32)]
    out_specs = [pl.BlockSpec((1, tm, D_MODEL), lambda i, j: (i, j, 0))]
    operands = [*branches, pg, x, mod, norm_g, *weights]
    if cast_next is not None:
        cast = _cast_plan(cast_next, layer + 1, b * tiles, lambda i, j: i * tiles + j)
        operands += cast.operands
        in_specs += cast.in_specs
        out_shape += cast.out_shape
        out_specs += cast.out_specs
    out = pl.pallas_call(
        _post_kernel,
        out_shape=tuple(out_shape),
        grid=(b, tiles),
        in_specs=in_specs,
        out_specs=tuple(out_specs),
        compiler_params=_params("arbitrary", "arbitrary"),
        name="merge_ffn",
    )(*operands)
    if cast_next is None:
        return out[0]
    return (out[0],) + _layer_params(out[1:])


class _CastPlan(NamedTuple):
    operands: list
    in_specs: list
    out_shape: list
    out_specs: list


def _cast_plan(stacked, layer, steps, step_of):
    w_in, w_branch, w_o, w_up, w_down, c_t, w_mod, b_mod = stacked
    sources = (w_in, w_branch.reshape(DEPTH, N_BRANCH * MIX, D_MODEL), w_o, w_up, w_down)
    plan = _CastPlan([], [], [], [])
    for src in sources:
        rows, width = src.shape[1:]
        share = 1 if (rows // steps) % 16 == 0 else 2
        height = rows * share // steps

        def src_index(*idx, share=share):
            return (layer, step_of(*idx) // share, 0)

        def dst_index(*idx, share=share):
            return (0, step_of(*idx) // share, 0)

        plan.operands.append(src)
        plan.in_specs.append(pl.BlockSpec((1, height, width), src_index))
        for out_width in ((PA_WIDTH, PG_WIDTH) if src is w_in else (width,)):
            plan.out_shape.append(jax.ShapeDtypeStruct((1, rows, out_width), BF16))
            plan.out_specs.append(pl.BlockSpec((1, height, out_width), dst_index))

    slab = D_MODEL // steps
    n_out = w_mod.shape[-1]
    plan.operands.extend([c_t, w_mod, b_mod.reshape(DEPTH, 1, n_out)])
    plan.in_specs.extend([
        pl.BlockSpec((slab, MOD_ROWS), lambda *idx: (step_of(*idx), 0)),
        pl.BlockSpec((1, slab, n_out), lambda *idx: (layer, step_of(*idx), 0)),
        pl.BlockSpec((1, 1, n_out), lambda *idx: (layer, 0, 0))])
    plan.out_shape.append(jax.ShapeDtypeStruct((1, MOD_ROWS, n_out), F32))
    plan.out_specs.append(pl.BlockSpec((1, MOD_ROWS, n_out), lambda *idx: (0, 0, 0)))
    return plan


PREP_INPUTS = 8


def _prep_init(bm_ref, mod_o, first):
    @pl.when(first)
    def _():
        mod_o[0] = jnp.broadcast_to(bm_ref[0], mod_o.shape[1:])


def _prep_slabs(win_f, wb_f, wo_f, wup_f, wdn_f, ct_ref, wm_ref, bm_ref,
                wa_o, wg_o, wb_o, wo_o, wup_o, wdn_o, mod_o):
    wa_o[0] = win_f[0, :, :PA_WIDTH].astype(BF16)
    wg_o[0] = win_f[0, :, PA_WIDTH:].astype(BF16)
    for src, dst in ((wb_f, wb_o), (wo_f, wo_o), (wup_f, wup_o), (wdn_f, wdn_o)):
        dst[0] = src[0].astype(BF16)
    a = _silu(ct_ref[...]).astype(BF16)
    mod_o[0] += lax.dot_general(a, wm_ref[0].astype(BF16), (((0,), (0,)), ((), ())),
                                preferred_element_type=F32)


def _layer_params(prep_outputs):
    wa, wg, wb, wo, wup, wdn, mod = prep_outputs
    return ((wa, wg, wb.reshape(1, N_BRANCH, MIX, D_MODEL), wo, wup, wdn),
            mod.reshape(1, MOD_ROWS, 6, D_MODEL))


def _prep_kernel(*refs):
    _prep_init(refs[PREP_INPUTS - 1], refs[-1], pl.program_id(0) == 0)
    _prep_slabs(*refs)


def _prep_layer(stacked, layer):
    steps = 32
    plan = _cast_plan(stacked, layer, steps, lambda s: s)
    out = pl.pallas_call(
        _prep_kernel,
        out_shape=tuple(plan.out_shape),
        grid=(steps,),
        in_specs=plan.in_specs,
        out_specs=tuple(plan.out_specs),
        compiler_params=_params("arbitrary"),
        name="prep_layer",
    )(*plan.operands)
    return _layer_params(out)


def _rope_tables(n):
    rows = np.repeat(np.arange(n // GRID_W, dtype=np.float64), GRID_W)
    cols = np.tile(np.arange(GRID_W, dtype=np.float64), n // GRID_W)
    lane = np.arange(MIX)
    in_head = lane % RET_HEAD_DIM
    pos = np.where((in_head < RET_HEAD_DIM // 2)[None, :], rows[:, None], cols[:, None])
    quarter = RET_HEAD_DIM // 4
    freq = ROPE_BASE ** (-np.arange(quarter, dtype=np.float64) / quarter)
    ang = pos * freq[lane % quarter][None, :]
    sign = np.where((lane % (2 * quarter)) < quarter, -1.0, 1.0)[None, :]
    return np.cos(ang).astype(np.float32), (np.sin(ang) * sign).astype(np.float32)


def _dft_tables(n):
    m = DFT_BLOCK
    r_cnt = n // m
    k = np.arange(m, dtype=np.int64)
    ang = 2.0 * np.pi * ((k[:, None] * k[None, :]) % m) / m
    dft = np.concatenate([np.cos(ang), np.sin(ang)], axis=1).astype(np.float32)
    angt = 2.0 * np.pi * (np.arange(r_cnt)[:, None] * k[None, :]) / n
    tw = np.stack([np.cos(angt), np.sin(angt)])[..., None].repeat(LANES, axis=-1).astype(np.float32)
    ch = np.arange(FFT_GROUP_DIM, dtype=np.int64)
    angc = 2.0 * np.pi * ((ch[:, None] * ch[None, :]) % FFT_GROUP_DIM) / FFT_GROUP_DIM
    eye = np.eye(FFT_GROUPS)
    cc = np.kron(eye, np.cos(angc)).astype(np.float32)
    ssn = np.kron(eye, -np.sin(angc)).astype(np.float32)
    return dft, tw, cc, ssn


def _pool_norm(n):
    t = np.arange(n)
    cols = []
    for w in POOL_WINDOWS:
        cnt = np.clip(t - w // 2 + w, 0, n) - np.clip(t - w // 2, 0, n)
        inv = 1.0 / cnt
        assert np.all(cnt[EDGE_ROWS:n - EDGE_ROWS] == w)
        col = np.stack([inv[:EDGE_ROWS], np.full(EDGE_ROWS, 1.0 / w), inv[n - EDGE_ROWS:]])
        cols.append(np.repeat(col[:, :, None], POOL_GROUP, axis=2))
    return np.concatenate(cols, axis=2).astype(np.float32)


def _block_diag(w):
    nl, g, d, _ = w.shape
    eye = jnp.eye(g, dtype=w.dtype)
    return (eye[None, :, None, :, None] * w[:, :, :, None, :]).reshape(nl, g * d, g * d)


def kernel(x, c, ctx, c_ctx, w_mod, b_mod, norm_g, w_in, ret_decay, conv_w, pool_w, pool_scale,
           w_branch, w_o, ffn_w_up, ffn_w_down):
    b, n, _ = x.shape
    n_ctx = ctx.shape[1]

    c_t = jnp.concatenate(
        [c, c_ctx[None, :], jnp.zeros((MOD_ROWS - b - 1, D_MODEL), F32)], axis=0).T
    ctx_row = b

    cos_l, sin_l = (jnp.asarray(t) for t in _rope_tables(n))
    cos_c, sin_c = jnp.ones((b * n_ctx, MIX), F32), jnp.zeros((b * n_ctx, MIX), F32)
    ctx = ctx.reshape(1, b * n_ctx, D_MODEL)

    def per_sample(t):
        return t.reshape(b, n_ctx, t.shape[-1])

    def flat(t):
        return t.reshape(1, b * n_ctx, t.shape[-1]) if t.shape[1] == n_ctx else t

    def fourier_tables(length):
        dft, tw, cc, ssn = _dft_tables(length)
        return (jnp.asarray(cc).astype(BF16), jnp.asarray(ssn).astype(BF16), jnp.asarray(tw),
                jnp.asarray(dft).astype(BF16))

    tabs_l, tabs_c = fourier_tables(n), fourier_tables(n_ctx)
    norm_l, norm_c = jnp.asarray(_pool_norm(n)), jnp.asarray(_pool_norm(n_ctx))
    avg = jnp.asarray(np.kron(np.eye(RET_HEADS), np.full((RET_HEAD_DIM, RET_HEAD_DIM),
                                                         1.0 / RET_HEAD_DIM)), F32).astype(BF16)
    s_zero = jnp.zeros((b, 2, MIX, MIX), F32)

    w_stacked = (w_in, w_branch, w_o, ffn_w_up, ffn_w_down, c_t, w_mod, b_mod)
    w_layer, mod = _prep_layer(w_stacked, 0)
    pool_bd = _block_diag(pool_w).astype(BF16)
    rd_lane = jnp.repeat(ret_decay.astype(F32), RET_HEAD_DIM, axis=2)
    ps = pool_scale.reshape(DEPTH, 1, MIX)

    for l in range(DEPTH):
        need_ctx = l < DEPTH - 1
        wa, wg = w_layer[:2]
        if need_ctx:
            h_c, pa_c = _in_proj(ctx, mod, norm_g, wa, cos_c, sin_c, l, ctx_row, tm=CTX_TILE)
            mixed_c = _mixer(per_sample(h_c), pa_c, wg, rd_lane, s_zero, avg, conv_w,
                             pool_bd, ps, norm_c, *tabs_c, l)
            s_ctx = mixed_c[2]
            pg_c, ret_c, _, conv_c, pool_c, four_c = (flat(t) for t in mixed_c)
        else:
            _, qkv_c = _in_proj(ctx, mod, norm_g, wa, cos_c, sin_c, l, ctx_row, tm=CTX_TILE,
                                pa_width=3 * MIX)
            s_ctx = _final_states(qkv_c, rd_lane, s_zero, l)

        h_l, pa_l = _in_proj(x, mod, norm_g, wa, cos_l, sin_l, l, None, tm=1024)
        pg_l, ret_l, _, conv_l, pool_l, four_l = _mixer(
            h_l, pa_l, wg, rd_lane, s_ctx, avg, conv_w, pool_bd, ps, norm_l, *tabs_l, l)
        if need_ctx:
            ctx = _post((ret_c, conv_c, four_c, pool_c), pg_c, ctx, mod, norm_g, w_layer[2:],
                        l, ctx_row, tm=512)
            x, w_layer, mod = _post((ret_l, conv_l, four_l, pool_l), pg_l, x, mod, norm_g,
                                    w_layer[2:], l, None, tm=512, cast_next=w_stacked)
        else:
            x = _post((ret_l, conv_l, four_l, pool_l), pg_l, x, mod, norm_g, w_layer[2:],
                      l, None, tm=512)
    return x
```

```python
import cmath
import functools
import math
from typing import NamedTuple

import numpy as np
import jax
import jax.numpy as jnp
from jax import lax
from jax.experimental import pallas as pl
from jax.experimental.pallas import tpu as pltpu

D_MODEL = 1024
DEPTH = 4
GRID_W = 64
MIX = 256
N_BRANCH = 4
RET_HEADS = 4
RET_HEAD_DIM = MIX // RET_HEADS
POOL_WINDOWS = (2, 4, 8, 16)
POOL_GROUP = MIX // len(POOL_WINDOWS)
FFT_GROUPS = 4
FFT_GROUP_DIM = MIX // FFT_GROUPS
D_FF = 2816
PA_WIDTH = 9 * MIX
PG_WIDTH = N_BRANCH * D_MODEL
ROPE_BASE = 10000.0
EPS = 1e-6

LANES = 128
RET_CHUNK = 256
RET_GROUP = 4
DFT_BLOCK = 256
SEQ_PAD = 16
EDGE_ROWS = 8
GATE_ROWS_MXU_STEP = 512
GATE_ROWS_VPU_STEP = 256
GATE_COLS = 256
CONV_ROWS = 64
CTX_TILE = 1024
IN_ROWS = 256
MOD_ROWS = 16
VMEM_LIMIT = 60 * 1024 * 1024

F32 = jnp.float32
BF16 = jnp.bfloat16


def _dot(a, b):
    return jnp.dot(a, b, preferred_element_type=F32)


def _rms(xf, g):
    ms = jnp.mean(xf * xf, axis=-1, keepdims=True)
    return xf * lax.rsqrt(ms + EPS) * g


def _sigmoid(x):
    return 0.5 * jnp.tanh(0.5 * x) + 0.5


def _silu(x):
    return x * _sigmoid(x)


def _const_spec(shape):
    zeros = (0,) * len(shape)
    return pl.BlockSpec(shape, lambda *_: zeros, pipeline_mode=pl.Buffered(1))


def _layer_spec(shape, layer):
    tail = (0,) * (len(shape) - 1)
    return pl.BlockSpec((1,) + tuple(shape[1:]), lambda *_: (layer,) + tail,
                        pipeline_mode=pl.Buffered(1))


def _mod_spec(row):
    return pl.BlockSpec((1, 1, 6, D_MODEL),
                        lambda i, j: (0, i if row is None else row, 0, 0))


def _params(*sem):
    return pltpu.CompilerParams(dimension_semantics=sem, vmem_limit_bytes=VMEM_LIMIT)


def _in_kernel(x_ref, mod_ref, g_ref, w_ref, cos_ref, sin_ref, h_ref, pa_ref):
    mod = mod_ref[0, 0]
    gain = g_ref[0, 0:1] * (1.0 + mod[1:2])
    tm = x_ref.shape[1]
    sub = min(tm, IN_ROWS)
    first_of_pair = (lax.broadcasted_iota(jnp.int32, (sub, MIX), 1) & 16) == 0

    def sub_tile(rows):
        hb = (_rms(x_ref[0, rows, :], gain) + mod[0:1]).astype(BF16)
        h_ref[0, rows, :] = hb
        yield
        cos, sin = cos_ref[rows, :], sin_ref[rows, :]

        def rope(t, scale):
            partner = jnp.where(first_of_pair, pltpu.roll(t, MIX - 16, 1), pltpu.roll(t, 16, 1))
            return (t * cos + partner * sin) * scale

        for g0 in range(0, pa_ref.shape[0], 3):
            res = _dot(hb, w_ref[0, :, g0 * MIX:(g0 + 3) * MIX])
            for g in range(3):
                t = res[:, g * MIX:(g + 1) * MIX]
                if g0 + g == 0:
                    t = rope(t, RET_HEAD_DIM ** -0.5)
                elif g0 + g == 1:
                    t = rope(t, 1.0)
                pa_ref[g0 + g, rows, :] = t.astype(BF16)
            yield

    tiles = [sub_tile(pl.ds(r0, sub)) for r0 in range(0, tm, sub)]
    next(tiles[0])
    for idx, cur in enumerate(tiles):
        next(cur)
        if idx + 1 < len(tiles):
            next(tiles[idx + 1])
        for _ in cur:
            pass


def _in_proj(x, mod, norm_g, w_in, cos, sin, layer, mod_row, tm, pa_width=PA_WIDTH):
    b, n, _ = x.shape
    tiles = n // tm
    return pl.pallas_call(
        _in_kernel,
        out_shape=(jax.ShapeDtypeStruct((b, n, D_MODEL), BF16),
                   jax.ShapeDtypeStruct((pa_width // MIX, b * n, MIX), BF16)),
        grid=(b, tiles),
        in_specs=[
            pl.BlockSpec((1, tm, D_MODEL), lambda i, j: (i, j, 0)),
            _mod_spec(mod_row),
            _layer_spec(norm_g.shape, layer),
            pl.BlockSpec((1, D_MODEL, pa_width), lambda i, j: (0, 0, 0),
                         pipeline_mode=pl.Buffered(1)),
            pl.BlockSpec((tm, MIX), lambda i, j: (j, 0)),
            pl.BlockSpec((tm, MIX), lambda i, j: (j, 0)),
        ],
        out_specs=(pl.BlockSpec((1, tm, D_MODEL), lambda i, j: (i, j, 0)),
                   pl.BlockSpec((pa_width // MIX, tm, MIX), lambda i, j: (0, i * tiles + j, 0))),
        compiler_params=_params("parallel", "parallel"),
        name="in_proj",
    )(x, mod, norm_g, w_in, cos, sin)


def _retention_states(k_ref, v_ref, rd_ref, s0_ref, sout_ref, sf_s, sb_s, dm_s, dec_s, *, n,
                      fill=lambda: None):
    c_len = RET_CHUNK
    nc = n // c_len

    @pl.when(pl.program_id(0) == 0)
    def _decay_tables():
        rd = rd_ref[0]
        lg = jnp.minimum(rd, 0.0) - jnp.log1p(jnp.exp(-jnp.abs(rd)))
        lgf, lgb = lg[0:1], lg[1:2]
        idx = lax.broadcasted_iota(jnp.int32, (c_len, MIX), 0).astype(F32)
        dec_s[0] = jnp.exp(lgf * (idx + 1.0))
        dec_s[1] = jnp.exp(lgf * (c_len - 1.0 - idx))
        dec_s[2] = jnp.exp(lgb * (c_len - idx))
        dec_s[3] = jnp.exp(lgb * idx)
        dec_s[4] = jnp.exp(lgf * float(c_len)) + 0.0 * idx
        dec_s[5] = jnp.exp(lgb * float(c_len)) + 0.0 * idx
        ii = lax.broadcasted_iota(jnp.int32, (c_len, c_len), 0)
        jj = lax.broadcasted_iota(jnp.int32, (c_len, c_len), 1)
        diff = (ii - jj).astype(F32)
        for h in range(RET_HEADS):
            lo = h * RET_HEAD_DIM
            rate = jnp.where(diff >= 0.0, lgf[:, lo:lo + 1], -lgb[:, lo:lo + 1])
            dm_s[h] = jnp.exp(rate * diff)

    row_head = lax.broadcasted_iota(jnp.int32, (MIX, MIX), 0) // RET_HEAD_DIM
    col_head = lax.broadcasted_iota(jnp.int32, (MIX, MIX), 1) // RET_HEAD_DIM
    block_diag = row_head == col_head
    tn_dims = (((0,), (0,)), ((), ()))

    def advance(state, c, key_decay, chunk_decay):
        rows = pl.ds(c * c_len, c_len)
        kd = (k_ref[0, rows, :].astype(F32) * key_decay).astype(BF16)
        upd = lax.dot_general(kd, v_ref[0, rows, :], tn_dims, preferred_element_type=F32)
        return state * chunk_decay + jnp.where(block_diag, upd, 0.0)

    fwd, bwd = s0_ref[0, 0], s0_ref[0, 1]
    for i in range(nc):
        fill()
        sf_s[i] = fwd.astype(BF16)
        sb_s[nc - 1 - i] = bwd.astype(BF16)
        fwd = advance(fwd, i, dec_s[1], dec_s[4])
        bwd = advance(bwd, nc - 1 - i, dec_s[3], dec_s[5])
    sout_ref[0, 0] = fwd
    sout_ref[0, 1] = bwd


def _retention_step(q_ref, k_ref, v_ref, g_ref, rd_ref, s0_ref, avg_ref, o_ref, sout_ref,
                    sf_s, sb_s, dm_s, dec_s, *, n, fill):
    c_len = RET_CHUNK
    nc = n // c_len
    head = lax.broadcasted_iota(jnp.int32, (c_len, MIX), 1) // RET_HEAD_DIM
    _retention_states(k_ref, v_ref, rd_ref, s0_ref, sout_ref, sf_s, sb_s, dm_s, dec_s, n=n,
                      fill=fill)

    nt_dims = (((1,), (1,)), ((), ()))
    avg = avg_ref[...]
    for c0 in range(0, nc, RET_GROUP):
        group = range(c0, min(c0 + RET_GROUP, nc))
        scores, cross = {}, {}
        for c in group:
            fill()
            rows = pl.ds(c * c_len, c_len)
            q, k = q_ref[0, rows, :], k_ref[0, rows, :]
            qf = q.astype(F32)
            cross[c] = (_dot((qf * dec_s[0]).astype(BF16), sf_s[c])
                        + _dot((qf * dec_s[2]).astype(BF16), sb_s[c]))
            fill()
            scores[c] = [lax.dot_general(jnp.where(head == h, q, jnp.zeros_like(q)), k, nt_dims,
                                         preferred_element_type=F32) for h in range(RET_HEADS)]
        ys = {}
        for c in group:
            v = v_ref[0, pl.ds(c * c_len, c_len), :]
            probs = [(scores[c][h] * dm_s[h]).astype(BF16) for h in range(RET_HEADS)]
            vals = [jnp.where(head == h, v, jnp.zeros_like(v)) for h in range(RET_HEADS)]
            ys[c] = cross[c] + _dot(jnp.concatenate(probs, axis=1), jnp.concatenate(vals, axis=0))
        for c in group:
            fill()
            rows = pl.ds(c * c_len, c_len)
            d = ys[c] - _dot(ys[c].astype(BF16), avg)
            yn = d * lax.rsqrt(_dot((d * d).astype(BF16), avg) + EPS)
            o_ref[0, rows, :] = (yn * _silu(g_ref[0, rows, :].astype(F32))).astype(BF16)


def _padded_rows(ref, r0, rows, n):
    zeros = jnp.zeros((SEQ_PAD, MIX), ref.dtype)
    top = zeros if r0 == 0 else ref[0, r0 - SEQ_PAD:r0, :]
    bot = zeros if r0 + rows == n else ref[0, r0 + rows:r0 + rows + SEQ_PAD, :]
    return jnp.concatenate([top, ref[0, r0:r0 + rows, :], bot], axis=0).astype(F32)


def _shift_rows(t, s):
    return pltpu.roll(t, s % t.shape[0], 0)


def _conv_step(cb_ref, cg_ref, cx_ref, cw_ref, conv_ref, *, n, fill):
    rows = min(n, CONV_ROWS)
    main = slice(SEQ_PAD, SEQ_PAD + rows)
    cw = cw_ref[0]
    for r0 in range(0, n, rows):
        fill()
        u = _padded_rows(cg_ref, r0, rows, n) * _padded_rows(cx_ref, r0, rows, n)
        conv = _shift_rows(u, 1) * cw[0:1] + u * cw[1:2] + _shift_rows(u, -1) * cw[2:3]
        conv_ref[0, r0:r0 + rows, :] = (
            cb_ref[0, r0:r0 + rows, :].astype(F32) * conv[main]).astype(BF16)


def _pool_step(pu_ref, pw_ref, ps_ref, norm_ref, pool_ref, *, n, fill):
    rows = min(n, CONV_ROWS)
    main = slice(SEQ_PAD, SEQ_PAD + rows)
    in_first_group = lax.broadcasted_iota(jnp.int32, (rows, LANES), 1) < POOL_GROUP
    for r0 in range(0, n, rows):
        fill()
        e_all = _padded_rows(pu_ref, r0, rows, n)
        halves = []
        for half in range(MIX // LANES):
            e = e_all[:, half * LANES:(half + 1) * LANES]
            c = e + _shift_rows(e, 1)
            sums = [c]
            for w in POOL_WINDOWS[1:2 * half + 2]:
                c = _shift_rows(c, w // 4) + _shift_rows(c, -(w // 4))
                sums.append(c)
            halves.append(jnp.where(in_first_group, sums[-2][main], sums[-1][main]))
        win = jnp.concatenate(halves, axis=1)
        mean = win * norm_ref[1, 0:1]
        if r0 == 0:
            mean = jnp.concatenate([win[:EDGE_ROWS] * norm_ref[0], mean[EDGE_ROWS:]], axis=0)
        if r0 + rows == n:
            mean = jnp.concatenate([mean[:-EDGE_ROWS], win[-EDGE_ROWS:] * norm_ref[2]], axis=0)
        pooled = (mean - e_all[main]).astype(BF16)
        pool_ref[0, r0:r0 + rows, :] = (_dot(pooled, pw_ref[0]) * ps_ref[0]).astype(BF16)


def _times_const(z, w):
    re, im = z

    def lin(a, ca, b, cb):
        acc = None
        for t, coef in ((a, ca), (b, cb)):
            if abs(coef) < 1e-12:
                continue
            unit = abs(abs(coef) - 1.0) < 1e-12
            term = t if unit else t * abs(coef)
            if acc is None:
                acc = term if coef > 0 else -term
            else:
                acc = acc + term if coef > 0 else acc - term
        return acc

    return lin(re, w.real, im, -w.imag), lin(re, w.imag, im, w.real)


def _dft_blocks(xs):
    count = len(xs)
    if count == 1:
        return xs
    even, odd = _dft_blocks(xs[0::2]), _dft_blocks(xs[1::2])
    out = [None] * count
    for k in range(count // 2):
        t = _times_const(odd[k], cmath.exp(-2j * math.pi * k / count))
        out[k] = (even[k][0] + t[0], even[k][1] + t[1])
        out[k + count // 2] = (even[k][0] - t[0], even[k][1] - t[1])
    return out


def _fourier_step(u_ref, cc_ref, ssn_ref, tw_ref, dft_ref, o_ref, spec_s, *, n, fill):
    m = DFT_BLOCK
    r_cnt = n // m
    scale = 1.0 / math.sqrt(n * FFT_GROUP_DIM)
    blocks = []
    for r in range(r_cnt):
        u = u_ref[0, r * m:(r + 1) * m, :]
        blocks.append((_dot(u, cc_ref[...]), _dot(u, ssn_ref[...])))
        fill()
    spectra = _dft_blocks(blocks)
    for k1 in range(r_cnt):
        fill()
        re, im = spectra[k1]
        if k1 > 0:
            tc = jnp.concatenate([tw_ref[0, k1]] * (MIX // LANES), axis=1)
            ts = jnp.concatenate([tw_ref[1, k1]] * (MIX // LANES), axis=1)
            re, im = re * tc + im * ts, im * tc - re * ts
        fill()
        rhs = jnp.concatenate([re.astype(BF16), im.astype(BF16)], axis=0)
        block = _dot(dft_ref[...], rhs) * scale
        fill()
        for half in range(MIX // LANES):
            spec_s[half, pl.ds(k1, m, stride=r_cnt), :] = block[:, half * LANES:(half + 1) * LANES]
    o_ref[0] = jnp.concatenate([spec_s[half] for half in range(MIX // LANES)],
                               axis=1).astype(BF16)


def _mixer_kernel(h_ref, wg_ref, a_ref, b_ref, c_ref, d_ref, rd_ref, s0_ref, avg_ref,
                  cw_ref, pw_ref, ps_ref, norm_ref, cc_ref, ssn_ref, tw_ref, dft_ref,
                  pg_ref, ret_ref, sout_ref, conv_ref, pool_ref, four_ref,
                  sf_s, sb_s, dm_s, dec_s, spec_s, *, n):
    step = pl.program_id(1)

    def gate_filler(j, row_chunk):
        rows = min(n, row_chunk)
        todo = [(r0, c0) for r0 in range(0, n, rows) for c0 in range(0, D_MODEL, GATE_COLS)]
        calls_per_piece = max(1, (n // CONV_ROWS) // len(todo))
        pieces = iter(todo)
        calls = iter(range(n))

        def emit():
            r0, c0 = next(pieces, (None, None))
            if r0 is not None:
                w = wg_ref[0, :, j * D_MODEL + c0:j * D_MODEL + c0 + GATE_COLS]
                gate = _sigmoid(_dot(h_ref[0, r0:r0 + rows, :], w))
                pg_ref[0, r0:r0 + rows, c0:c0 + GATE_COLS] = gate.astype(BF16)

        def fill():
            if next(calls) % calls_per_piece == 0:
                emit()

        def drain():
            for _ in todo:
                emit()

        return fill, drain

    @pl.when(step == 0)
    def _():
        fill, drain = gate_filler(0, GATE_ROWS_VPU_STEP)
        _conv_step(a_ref, b_ref, c_ref, cw_ref, conv_ref, n=n, fill=fill)
        drain()

    @pl.when(step == 1)
    def _():
        fill, drain = gate_filler(1, GATE_ROWS_MXU_STEP)
        _fourier_step(a_ref, cc_ref, ssn_ref, tw_ref, dft_ref, four_ref, spec_s, n=n, fill=fill)
        drain()

    @pl.when(step == 2)
    def _():
        fill, drain = gate_filler(2, GATE_ROWS_VPU_STEP)
        _pool_step(d_ref, pw_ref, ps_ref, norm_ref, pool_ref, n=n, fill=fill)
        drain()

    @pl.when(step == 3)
    def _():
        fill, drain = gate_filler(3, GATE_ROWS_MXU_STEP)
        _retention_step(a_ref, b_ref, c_ref, d_ref, rd_ref, s0_ref, avg_ref, ret_ref, sout_ref,
                        sf_s, sb_s, dm_s, dec_s, n=n, fill=fill)
        drain()


_SLOT_COLUMNS = ((4, 7, 7, 0), (5, 5, 5, 1), (6, 6, 6, 2), (8, 8, 8, 3))


def _mixer(h, pa, wg, rd_lane, s0, avg, conv_w, pool_bd, pool_scale, pool_norm,
           cc, ssn, tw, dft, layer):
    b, n, _ = h.shape
    nc = n // RET_CHUNK

    def slot(cols):
        def index(i, j):
            col = jnp.where(j == 0, cols[0],
                            jnp.where(j == 1, cols[1], jnp.where(j == 2, cols[2], cols[3])))
            return (col, i, 0)
        return pl.BlockSpec((1, n, MIX), index)

    seq_out = pl.BlockSpec((1, n, MIX), lambda i, j: (i, 0, 0))
    state = pl.BlockSpec((1, 2, MIX, MIX), lambda i, j: (i, 0, 0, 0))
    return pl.pallas_call(
        functools.partial(_mixer_kernel, n=n),
        out_shape=(jax.ShapeDtypeStruct((b, n, PG_WIDTH), BF16),
                   jax.ShapeDtypeStruct((b, n, MIX), BF16),
                   jax.ShapeDtypeStruct((b, 2, MIX, MIX), F32),
                   jax.ShapeDtypeStruct((b, n, MIX), BF16),
                   jax.ShapeDtypeStruct((b, n, MIX), BF16),
                   jax.ShapeDtypeStruct((b, n, MIX), BF16)),
        grid=(b, 4),
        in_specs=[pl.BlockSpec((1, n, D_MODEL), lambda i, j: (i, 0, 0)),
                  _layer_spec(wg.shape, 0),
                  slot(_SLOT_COLUMNS[0]), slot(_SLOT_COLUMNS[1]),
                  slot(_SLOT_COLUMNS[2]), slot(_SLOT_COLUMNS[3]),
                  _layer_spec(rd_lane.shape, layer), state, _const_spec(avg.shape),
                  _layer_spec(conv_w.shape, layer), _layer_spec(pool_bd.shape, layer),
                  _layer_spec(pool_scale.shape, layer), _const_spec(pool_norm.shape),
                  _const_spec(cc.shape), _const_spec(ssn.shape), _const_spec(tw.shape),
                  _const_spec(dft.shape)],
        out_specs=(pl.BlockSpec((1, n, D_MODEL), lambda i, j: (i, 0, j)),
                   seq_out, state, seq_out, seq_out, seq_out),
        scratch_shapes=[pltpu.VMEM((nc, MIX, MIX), BF16), pltpu.VMEM((nc, MIX, MIX), BF16),
                        pltpu.VMEM((RET_HEADS, RET_CHUNK, RET_CHUNK), F32),
                        pltpu.VMEM((6, RET_CHUNK, MIX), F32),
                        pltpu.VMEM((MIX // LANES, n, LANES), F32)],
        compiler_params=_params("arbitrary", "arbitrary"),
        name="mixer",
    )(h, wg, pa, pa, pa, pa, rd_lane, s0, avg, conv_w, pool_bd, pool_scale, pool_norm,
      cc, ssn, tw, dft)


def _state_kernel(k_ref, v_ref, rd_ref, s0_ref, sout_ref, sf_s, sb_s, dm_s, dec_s, *, n):
    _retention_states(k_ref, v_ref, rd_ref, s0_ref, sout_ref, sf_s, sb_s, dm_s, dec_s, n=n)


def _final_states(pa, rd_lane, s0, layer):
    b = s0.shape[0]
    n = pa.shape[1] // b
    nc = n // RET_CHUNK
    state = pl.BlockSpec((1, 2, MIX, MIX), lambda i: (i, 0, 0, 0))
    return pl.pallas_call(
        functools.partial(_state_kernel, n=n),
        out_shape=jax.ShapeDtypeStruct((b, 2, MIX, MIX), F32),
        grid=(b,),
        in_specs=[pl.BlockSpec((1, n, MIX), lambda i: (1, i, 0)),
                  pl.BlockSpec((1, n, MIX), lambda i: (2, i, 0)),
                  _layer_spec(rd_lane.shape, layer), state],
        out_specs=state,
        scratch_shapes=[pltpu.VMEM((nc, MIX, MIX), BF16), pltpu.VMEM((nc, MIX, MIX), BF16),
                        pltpu.VMEM((RET_HEADS, RET_CHUNK, RET_CHUNK), F32),
                        pltpu.VMEM((6, RET_CHUNK, MIX), F32)],
        compiler_params=_params("arbitrary"),
        name="final_states",
    )(pa, pa, rd_lane, s0)


FFN_SPLITS = (0, 768, 1536, 2304, D_FF)
POST_ROWS = 256


def _post_kernel(b0_ref, b1_ref, b2_ref, b3_ref, pg_ref, x_ref, mod_ref, g_ref,
                 wb_ref, wo_ref, wup_ref, wdn_ref, *rest):
    if len(rest) == 1:
        o_ref, = rest
        cast_jobs = []
    else:
        o_ref = rest[PREP_INPUTS]
        prep_refs = rest[:PREP_INPUTS] + rest[PREP_INPUTS + 1:]
        _prep_init(prep_refs[PREP_INPUTS - 1], prep_refs[-1],
                   (pl.program_id(0) == 0) & (pl.program_id(1) == 0))
        cast_jobs = [functools.partial(_prep_slabs, *prep_refs)]
    mod = mod_ref[0, 0]
    g = g_ref[0]

    def sub_tile(rows):
        merged = None
        for i, br in enumerate((b0_ref, b1_ref, b2_ref, b3_ref)):
            gate = pg_ref[0, rows, i * D_MODEL:(i + 1) * D_MODEL].astype(F32)
            t = gate * _dot(br[0, rows, :], wb_ref[0, i])
            merged = t if merged is None else merged + t
            if i % 2:
                yield "mix"
        mix = _dot(merged.astype(BF16), wo_ref[0])
        x1 = x_ref[0, rows, :] + _rms(mix, g[1:2] * mod[2:3])
        yield "mix"
        h = (_rms(x1, g[2:3] * (1.0 + mod[4:5])) + mod[3:4]).astype(BF16)
        yield "mix"
        ffn = None
        for lo, hi in zip(FFN_SPLITS[:-1], FFN_SPLITS[1:]):
            a = _dot(h, wup_ref[0, :, lo:hi])
            u = _dot(h, wup_ref[0, :, D_FF + lo:D_FF + hi])
            t = _dot((_silu(a) * u).astype(BF16), wdn_ref[0, lo:hi, :])
            ffn = t if ffn is None else ffn + t
            yield "ffn"
        o_ref[0, rows, :] = x1 + _rms(ffn, g[3:4] * mod[5:6])
        yield "done"

    tm = o_ref.shape[1]
    tiles = [sub_tile(pl.ds(r0, POST_ROWS)) for r0 in range(0, tm, POST_ROWS)]
    mix_pieces = ffn_chunks = len(FFN_SPLITS) - 1
    for _ in range(mix_pieces):
        next(tiles[0])
    for idx, cur in enumerate(tiles):
        for chunk in range(ffn_chunks):
            next(cur)
            if chunk == 0 and idx > 0:
                next(tiles[idx - 1])
            if idx + 1 < len(tiles):
                next(tiles[idx + 1])
            elif chunk >= 1 and cast_jobs:
                cast_jobs.pop()()
    next(tiles[-1])


def _post(branches, pg, x, mod, norm_g, weights, layer, mod_row, tm, cast_next=None):
    b, n, _ = x.shape
    tiles = n // tm
    br = pl.BlockSpec((1, tm, MIX), lambda i, j: (i, j, 0))
    in_specs = [br, br, br, br,
                pl.BlockSpec((1, tm, PG_WIDTH), lambda i, j: (i, j, 0)),
                pl.BlockSpec((1, tm, D_MODEL), lambda i, j: (i, j, 0)),
                _mod_spec(mod_row),
                _layer_spec(norm_g.shape, layer)]
    in_specs += [_layer_spec(w.shape, 0) for w in weights]
    out_shape = [jax.ShapeDtypeStruct((b, n, D_MODEL), F32)]
    out_specs = [pl.BlockSpec((1, tm, D_MODEL), lambda i, j: (i, j, 0))]
    operands = [*branches, pg, x, mod, norm_g, *weights]
    if cast_next is not None:
        cast = _cast_plan(cast_next, layer + 1, b * tiles, lambda i, j: i * tiles + j)
        operands += cast.operands
        in_specs += cast.in_specs
        out_shape += cast.out_shape
        out_specs += cast.out_specs
    out = pl.pallas_call(
        _post_kernel,
        out_shape=tuple(out_shape),
        grid=(b, tiles),
        in_specs=in_specs,
        out_specs=tuple(out_specs),
        compiler_params=_params("arbitrary", "arbitrary"),
        name="merge_ffn",
    )(*operands)
    if cast_next is None:
        return out[0]
    return (out[0],) + _layer_params(out[1:])


class _CastPlan(NamedTuple):
    operands: list
    in_specs: list
    out_shape: list
    out_specs: list


def _cast_plan(stacked, layer, steps, step_of):
    w_in, w_branch, w_o, w_up, w_down, c_t, w_mod, b_mod = stacked
    sources = (w_in, w_branch.reshape(DEPTH, N_BRANCH * MIX, D_MODEL), w_o, w_up, w_down)
    plan = _CastPlan([], [], [], [])
    for src in sources:
        rows, width = src.shape[1:]
        share = 1 if (rows // steps) % 16 == 0 else 2
        height = rows * share // steps

        def src_index(*idx, share=share):
            return (layer, step_of(*idx) // share, 0)

        def dst_index(*idx, share=share):
            return (0, step_of(*idx) // share, 0)

        plan.operands.append(src)
        plan.in_specs.append(pl.BlockSpec((1, height, width), src_index))
        for out_width in ((PA_WIDTH, PG_WIDTH) if src is w_in else (width,)):
            plan.out_shape.append(jax.ShapeDtypeStruct((1, rows, out_width), BF16))
            plan.out_specs.append(pl.BlockSpec((1, height, out_width), dst_index))

    slab = D_MODEL // steps
    n_out = w_mod.shape[-1]
    plan.operands.extend([c_t, w_mod, b_mod.reshape(DEPTH, 1, n_out)])
    plan.in_specs.extend([
        pl.BlockSpec((slab, MOD_ROWS), lambda *idx: (step_of(*idx), 0)),
        pl.BlockSpec((1, slab, n_out), lambda *idx: (layer, step_of(*idx), 0)),
        pl.BlockSpec((1, 1, n_out), lambda *idx: (layer, 0, 0))])
    plan.out_shape.append(jax.ShapeDtypeStruct((1, MOD_ROWS, n_out), F32))
    plan.out_specs.append(pl.BlockSpec((1, MOD_ROWS, n_out), lambda *idx: (0, 0, 0)))
    return plan


PREP_INPUTS = 8


def _prep_init(bm_ref, mod_o, first):
    @pl.when(first)
    def _():
        mod_o[0] = jnp.broadcast_to(bm_ref[0], mod_o.shape[1:])


def _prep_slabs(win_f, wb_f, wo_f, wup_f, wdn_f, ct_ref, wm_ref, bm_ref,
                wa_o, wg_o, wb_o, wo_o, wup_o, wdn_o, mod_o):
    wa_o[0] = win_f[0, :, :PA_WIDTH].astype(BF16)
    wg_o[0] = win_f[0, :, PA_WIDTH:].astype(BF16)
    for src, dst in ((wb_f, wb_o), (wo_f, wo_o), (wup_f, wup_o), (wdn_f, wdn_o)):
        dst[0] = src[0].astype(BF16)
    a = _silu(ct_ref[...]).astype(BF16)
    mod_o[0] += lax.dot_general(a, wm_ref[0].astype(BF16), (((0,), (0,)), ((), ())),
                                preferred_element_type=F32)


def _layer_params(prep_outputs):
    wa, wg, wb, wo, wup, wdn, mod = prep_outputs
    return ((wa, wg, wb.reshape(1, N_BRANCH, MIX, D_MODEL), wo, wup, wdn),
            mod.reshape(1, MOD_ROWS, 6, D_MODEL))


def _prep_kernel(*refs):
    _prep_init(refs[PREP_INPUTS - 1], refs[-1], pl.program_id(0) == 0)
    _prep_slabs(*refs)


def _prep_layer(stacked, layer):
    steps = 32
    plan = _cast_plan(stacked, layer, steps, lambda s: s)
    out = pl.pallas_call(
        _prep_kernel,
        out_shape=tuple(plan.out_shape),
        grid=(steps,),
        in_specs=plan.in_specs,
        out_specs=tuple(plan.out_specs),
        compiler_params=_params("arbitrary"),
        name="prep_layer",
    )(*plan.operands)
    return _layer_params(out)


def _rope_tables(n):
    rows = np.repeat(np.arange(n // GRID_W, dtype=np.float64), GRID_W)
    cols = np.tile(np.arange(GRID_W, dtype=np.float64), n // GRID_W)
    lane = np.arange(MIX)
    in_head = lane % RET_HEAD_DIM
    pos = np.where((in_head < RET_HEAD_DIM // 2)[None, :], rows[:, None], cols[:, None])
    quarter = RET_HEAD_DIM // 4
    freq = ROPE_BASE ** (-np.arange(quarter, dtype=np.float64) / quarter)
    ang = pos * freq[lane % quarter][None, :]
    sign = np.where((lane % (2 * quarter)) < quarter, -1.0, 1.0)[None, :]
    return np.cos(ang).astype(np.float32), (np.sin(ang) * sign).astype(np.float32)


def _dft_tables(n):
    m = DFT_BLOCK
    r_cnt = n // m
    k = np.arange(m, dtype=np.int64)
    ang = 2.0 * np.pi * ((k[:, None] * k[None, :]) % m) / m
    dft = np.concatenate([np.cos(ang), np.sin(ang)], axis=1).astype(np.float32)
    angt = 2.0 * np.pi * (np.arange(r_cnt)[:, None] * k[None, :]) / n
    tw = np.stack([np.cos(angt), np.sin(angt)])[..., None].repeat(LANES, axis=-1).astype(np.float32)
    ch = np.arange(FFT_GROUP_DIM, dtype=np.int64)
    angc = 2.0 * np.pi * ((ch[:, None] * ch[None, :]) % FFT_GROUP_DIM) / FFT_GROUP_DIM
    eye = np.eye(FFT_GROUPS)
    cc = np.kron(eye, np.cos(angc)).astype(np.float32)
    ssn = np.kron(eye, -np.sin(angc)).astype(np.float32)
    return dft, tw, cc, ssn


def _pool_norm(n):
    t = np.arange(n)
    cols = []
    for w in POOL_WINDOWS:
        cnt = np.clip(t - w // 2 + w, 0, n) - np.clip(t - w // 2, 0, n)
        inv = 1.0 / cnt
        assert np.all(cnt[EDGE_ROWS:n - EDGE_ROWS] == w)
        col = np.stack([inv[:EDGE_ROWS], np.full(EDGE_ROWS, 1.0 / w), inv[n - EDGE_ROWS:]])
        cols.append(np.repeat(col[:, :, None], POOL_GROUP, axis=2))
    return np.concatenate(cols, axis=2).astype(np.float32)


def _block_diag(w):
    nl, g, d, _ = w.shape
    eye = jnp.eye(g, dtype=w.dtype)
    return (eye[None, :, None, :, None] * w[:, :, :, None, :]).reshape(nl, g * d, g * d)


def kernel(x, c, ctx, c_ctx, w_mod, b_mod, norm_g, w_in, ret_decay, conv_w, pool_w, pool_scale,
           w_branch, w_o, ffn_w_up, ffn_w_down):
    b, n, _ = x.shape
    n_ctx = ctx.shape[1]

    c_t = jnp.concatenate(
        [c, c_ctx[None, :], jnp.zeros((MOD_ROWS - b - 1, D_MODEL), F32)], axis=0).T
    ctx_row = b

    cos_l, sin_l = (jnp.asarray(t) for t in _rope_tables(n))
    cos_c, sin_c = jnp.ones((b * n_ctx, MIX), F32), jnp.zeros((b * n_ctx, MIX), F32)
    ctx = ctx.reshape(1, b * n_ctx, D_MODEL)

    def per_sample(t):
        return t.reshape(b, n_ctx, t.shape[-1])

    def flat(t):
        return t.reshape(1, b * n_ctx, t.shape[-1]) if t.shape[1] == n_ctx else t

    def fourier_tables(length):
        dft, tw, cc, ssn = _dft_tables(length)
        return (jnp.asarray(cc).astype(BF16), jnp.asarray(ssn).astype(BF16), jnp.asarray(tw),
                jnp.asarray(dft).astype(BF16))

    tabs_l, tabs_c = fourier_tables(n), fourier_tables(n_ctx)
    norm_l, norm_c = jnp.asarray(_pool_norm(n)), jnp.asarray(_pool_norm(n_ctx))
    avg = jnp.asarray(np.kron(np.eye(RET_HEADS), np.full((RET_HEAD_DIM, RET_HEAD_DIM),
                                                         1.0 / RET_HEAD_DIM)), F32).astype(BF16)
    s_zero = jnp.zeros((b, 2, MIX, MIX), F32)

    w_stacked = (w_in, w_branch, w_o, ffn_w_up, ffn_w_down, c_t, w_mod, b_mod)
    w_layer, mod = _prep_layer(w_stacked, 0)
    pool_bd = _block_diag(pool_w).astype(BF16)
    rd_lane = jnp.repeat(ret_decay.astype(F32), RET_HEAD_DIM, axis=2)
    ps = pool_scale.reshape(DEPTH, 1, MIX)

    for l in range(DEPTH):
        need_ctx = l < DEPTH - 1
        wa, wg = w_layer[:2]
        if need_ctx:
            h_c, pa_c = _in_proj(ctx, mod, norm_g, wa, cos_c, sin_c, l, ctx_row, tm=CTX_TILE)
            mixed_c = _mixer(per_sample(h_c), pa_c, wg, rd_lane, s_zero, avg, conv_w,
                             pool_bd, ps, norm_c, *tabs_c, l)
            s_ctx = mixed_c[2]
            pg_c, ret_c, _, conv_c, pool_c, four_c = (flat(t) for t in mixed_c)
        else:
            _, qkv_c = _in_proj(ctx, mod, norm_g, wa, cos_c, sin_c, l, ctx_row, tm=CTX_TILE,
                                pa_width=3 * MIX)
            s_ctx = _final_states(qkv_c, rd_lane, s_zero, l)

        h_l, pa_l = _in_proj(x, mod, norm_g, wa, cos_l, sin_l, l, None, tm=1024)
        pg_l, ret_l, _, conv_l, pool_l, four_l = _mixer(
            h_l, pa_l, wg, rd_lane, s_ctx, avg, conv_w, pool_bd, ps, norm_l, *tabs_l, l)
        if need_ctx:
            ctx = _post((ret_c, conv_c, four_c, pool_c), pg_c, ctx, mod, norm_g, w_layer[2:],
                        l, ctx_row, tm=512)
            x, w_layer, mod = _post((ret_l, conv_l, four_l, pool_l), pg_l, x, mod, norm_g,
                                    w_layer[2:], l, None, tm=512, cast_next=w_stacked)
        else:
            x = _post((ret_l, conv_l, four_l, pool_l), pg_l, x, mod, norm_g, w_layer[2:],
                      l, None, tm=512)
    return x
```
